```python
import math
import jax, jax.numpy as jnp
from jax import lax
import numpy as np

D_MODEL = 1024
BATCH = 16
SEQ = 4096
DEPTH = 1

HEAD_DIM = 64
N_HEADS_FOX = 8
N_HEADS_DIL = 8
WIDTH_FOX = N_HEADS_FOX * HEAD_DIM
WIDTH_DIL = N_HEADS_DIL * HEAD_DIM
D_MIX = WIDTH_FOX + WIDTH_DIL
D_IN = 3 * WIDTH_FOX + N_HEADS_FOX + 3 * WIDTH_DIL
BLOCK = 128
DILATED_PATTERNS = ((128, 1), (512, 4), (2048, 16))
ROPE_THETA = 500000.0
ROPE_DIMS = HEAD_DIM // 4
D_FF = 2816
CONV_WIDTH = 3
DEEPNORM_ALPHA = (2.0 * DEPTH) ** 0.25
DEEPNORM_BETA = (8.0 * DEPTH) ** -0.25
LN_EPS = 1e-5
RMS_EPS = 1e-6

kernel_name = "fox_dilated_hybrid_deepnorm_block"


def _layer_norm(x, g, b):
    xf = x.astype(jnp.float32)
    mu = jnp.mean(xf, axis=-1, keepdims=True)
    var = jnp.mean(jnp.square(xf - mu), axis=-1, keepdims=True)
    y = (xf - mu) * lax.rsqrt(var + LN_EPS)
    return (y * g.astype(jnp.float32) + b.astype(jnp.float32)).astype(x.dtype)


def _head_rms_norm(o, gain):
    of = o.astype(jnp.float32)
    of = of * lax.rsqrt(jnp.mean(jnp.square(of), axis=-1, keepdims=True) + RMS_EPS)
    B, S, H, Dh = o.shape
    return (of.reshape(B, S, H * Dh) * gain.astype(jnp.float32)).astype(o.dtype)


def _partial_rotary(t, positions):
    half = ROPE_DIMS // 2
    freqs = ROPE_THETA ** (-jnp.arange(0, ROPE_DIMS, 2, dtype=jnp.float32) / ROPE_DIMS)
    ang = positions.astype(jnp.float32)[:, :, None] * freqs
    cos = jnp.cos(ang)[:, :, None, :]
    sin = jnp.sin(ang)[:, :, None, :]
    tf = t.astype(jnp.float32)
    t1, t2, rest = tf[..., :half], tf[..., half:ROPE_DIMS], tf[..., ROPE_DIMS:]
    rot = jnp.concatenate([t1 * cos - t2 * sin, t2 * cos + t1 * sin, rest], axis=-1)
    return rot.astype(t.dtype)


def _forgetting_attention(q, k, v, log_f):
    B, S, H, Dh = q.shape
    nb = S // BLOCK
    scale = Dh ** -0.5
    F = jnp.cumsum(log_f, axis=1).transpose(0, 2, 1)
    qh, kh, vh = (t.transpose(0, 2, 1, 3) for t in (q, k, v))
    q_blocks = jnp.moveaxis(qh.reshape(B, H, nb, BLOCK, Dh), 2, 0)
    F_blocks = jnp.moveaxis(F.reshape(B, H, nb, BLOCK), 2, 0)
    k_pos = jnp.arange(S)

    def one_block(args):
        qb, Fq, n = args
        s = jnp.einsum('bhqd,bhkd->bhqk', qb, kh).astype(jnp.float32) * scale
        s = s + Fq[..., :, None] - F[:, :, None, :]
        q_pos = n * BLOCK + jnp.arange(BLOCK)
        s = jnp.where(k_pos[None, :] <= q_pos[:, None], s, -jnp.inf)
        p = jax.nn.softmax(s, axis=-1)
        return jnp.einsum('bhqk,bhkd->bhqd', p.astype(vh.dtype), vh)

    o = lax.map(one_block, (q_blocks, F_blocks, jnp.arange(nb)))
    o = jnp.moveaxis(o, 0, 2).reshape(B, H, S, Dh)
    return o.transpose(0, 2, 1, 3)


def _dilated_pattern(q, k, v, dilation, steps):
    B, S, H, Dh = q.shape
    L = S // dilation
    BB = B * dilation
    scale = Dh ** -0.5

    def to_sub(t):
        return t.reshape(B, L, dilation, H, Dh).transpose(0, 2, 3, 1, 4).reshape(BB, H, L, Dh)

    nb = -(-L // BLOCK)
    Lp = nb * BLOCK
    pad = ((0, 0), (0, 0), (0, Lp - L), (0, 0))
    qs, ks, vs = (jnp.pad(to_sub(t), pad).reshape(BB, H, nb, BLOCK, Dh) for t in (q, k, v))
    blk_pad = ((0, 0), (0, 0), (1, 0), (0, 0), (0, 0))
    k_cat = jnp.concatenate([jnp.pad(ks, blk_pad)[:, :, :-1], ks], axis=3)
    v_cat = jnp.concatenate([jnp.pad(vs, blk_pad)[:, :, :-1], vs], axis=3)

    s = jnp.einsum('bhnqd,bhnkd->bhnqk', qs, k_cat).astype(jnp.float32) * scale
    qi = jnp.arange(BLOCK)[:, None]
    kj = jnp.arange(2 * BLOCK)[None, :]
    dist = qi + BLOCK - kj
    k_pos = jnp.arange(nb)[:, None, None] * BLOCK + kj - BLOCK
    mask = (dist >= 0) & (dist <= steps) & (k_pos >= 0)
    s = jnp.where(mask, s, -jnp.inf)
    m = jnp.max(s, axis=-1, keepdims=True)
    p = jnp.exp(s - m)
    l = jnp.sum(p, axis=-1, keepdims=True)
    o = jnp.einsum('bhnqk,bhnkd->bhnqd', (p / l).astype(v.dtype), v_cat)
    lse = (m + jnp.log(l))[..., 0]

    o = o.reshape(BB, H, Lp, Dh)[:, :, :L]
    o = o.reshape(B, dilation, H, L, Dh).transpose(0, 3, 1, 2, 4).reshape(B, S, H, Dh)
    lse = lse.reshape(BB, H, Lp)[:, :, :L]
    lse = lse.reshape(B, dilation, H, L).transpose(0, 3, 1, 2).reshape(B, S, H)
    return o, lse


def _dilated_attention(q, k, v):
    outs, lses = [], []
    for window, dilation in DILATED_PATTERNS:
        o, lse = _dilated_pattern(q, k, v, dilation, window // dilation)
        outs.append(o)
        lses.append(lse)
    w = jax.nn.softmax(jnp.stack(lses, axis=0), axis=0)
    o = jnp.sum(w[..., None] * jnp.stack(outs, axis=0).astype(jnp.float32), axis=0)
    return o.astype(q.dtype)


def _token_mixer(h, positions, w_in, b_fgate, gn_a, gn_b, w_out):
    B, S, _ = h.shape
    z = h @ w_in
    o0 = 0
    qa = z[..., o0:o0 + WIDTH_FOX]; o0 += WIDTH_FOX
    ka = z[..., o0:o0 + WIDTH_FOX]; o0 += WIDTH_FOX
    va = z[..., o0:o0 + WIDTH_FOX]; o0 += WIDTH_FOX
    fa = z[..., o0:o0 + N_HEADS_FOX]; o0 += N_HEADS_FOX
    qb = z[..., o0:o0 + WIDTH_DIL]; o0 += WIDTH_DIL
    kb = z[..., o0:o0 + WIDTH_DIL]; o0 += WIDTH_DIL
    vb = z[..., o0:o0 + WIDTH_DIL]

    heads = lambda t, H: t.reshape(B, S, H, HEAD_DIM)
    log_f = jax.nn.log_sigmoid((fa + b_fgate).astype(jnp.float32))
    oa = _forgetting_attention(heads(qa, N_HEADS_FOX), heads(ka, N_HEADS_FOX),
                               heads(va, N_HEADS_FOX), log_f)
    qb = _partial_rotary(heads(qb, N_HEADS_DIL), positions)
    kb = _partial_rotary(heads(kb, N_HEADS_DIL), positions)
    ob = _dilated_attention(qb, kb, heads(vb, N_HEADS_DIL))

    merged = jnp.concatenate([_head_rms_norm(oa, gn_a), _head_rms_norm(ob, gn_b)], axis=-1)
    return merged @ w_out


def _conv_ffn(h, w_up, conv_w, conv_b, w_down):
    u = h @ w_up
    up = jnp.pad(u, ((0, 0), (CONV_WIDTH - 1, 0), (0, 0)))
    S = h.shape[1]
    y = conv_b + sum(up[:, i:i + S] * conv_w[i] for i in range(CONV_WIDTH))
    a, g = jnp.split(y, 2, axis=-1)
    return (jax.nn.silu(g) * a) @ w_down


def setup_inputs(seed: int = 0) -> dict:
    key = jax.random.key(seed)
    ks = jax.random.split(key, 20)
    n = jax.random.normal
    f32 = jnp.float32
    x = n(ks[0], (BATCH, SEQ, D_MODEL), f32)
    c = n(ks[1], (BATCH, D_MODEL), f32)
    offset = jax.random.randint(ks[2], (BATCH, 1), 0, 1024, dtype=jnp.int32)
    positions = (offset + jnp.arange(SEQ, dtype=jnp.int32)[None, :]).astype(jnp.int32)
    w_ada = n(ks[3], (DEPTH, D_MODEL, 6 * D_MODEL), f32) * D_MODEL ** -0.5
    b_ada = 0.02 * n(ks[4], (DEPTH, 6 * D_MODEL), f32)
    w_in = n(ks[5], (DEPTH, D_MODEL, D_IN), f32) * D_MODEL ** -0.5
    b_fgate = jnp.linspace(1.0, 6.0, N_HEADS_FOX, dtype=f32)[None, :] + 0.1 * n(ks[6], (DEPTH, N_HEADS_FOX), f32)
    gn_a = 1.0 + 0.02 * n(ks[7], (DEPTH, WIDTH_FOX), f32)
    gn_b = 1.0 + 0.02 * n(ks[8], (DEPTH, WIDTH_DIL), f32)
    w_out = n(ks[9], (DEPTH, D_MIX, D_MODEL), f32) * D_MIX ** -0.5 * DEEPNORM_BETA
    ln1_g = 1.0 + 0.02 * n(ks[10], (DEPTH, D_MODEL), f32)
    ln1_b = 0.02 * n(ks[11], (DEPTH, D_MODEL), f32)
    w_up = n(ks[12], (DEPTH, D_MODEL, 2 * D_FF), f32) * D_MODEL ** -0.5
    conv_w = n(ks[13], (DEPTH, CONV_WIDTH, 2 * D_FF), f32) * CONV_WIDTH ** -0.5
    conv_b = 0.02 * n(ks[14], (DEPTH, 2 * D_FF), f32)
    w_down = n(ks[15], (DEPTH, D_FF, D_MODEL), f32) * D_FF ** -0.5 * DEEPNORM_BETA
    ln2_g = 1.0 + 0.02 * n(ks[16], (DEPTH, D_MODEL), f32)
    ln2_b = 0.02 * n(ks[17], (DEPTH, D_MODEL), f32)
    return {"x": x, "c": c, "positions": positions, "w_ada": w_ada, "b_ada": b_ada,
            "w_in": w_in, "b_fgate": b_fgate, "gn_a": gn_a, "gn_b": gn_b, "w_out": w_out,
            "ln1_g": ln1_g, "ln1_b": ln1_b, "w_up": w_up, "conv_w": conv_w, "conv_b": conv_b,
            "w_down": w_down, "ln2_g": ln2_g, "ln2_b": ln2_b}


def reference(x, c, positions, w_ada, b_ada, w_in, b_fgate, gn_a, gn_b, w_out,
              ln1_g, ln1_b, w_up, conv_w, conv_b, w_down, ln2_g, ln2_b):
    for l in range(DEPTH):
        ada = jax.nn.silu(c) @ w_ada[l] + b_ada[l]
        sh_a, sc_a, g_a, sh_f, sc_f, g_f = (t[:, None, :] for t in jnp.split(ada, 6, axis=-1))
        h = x * (1.0 + sc_a) + sh_a
        mix = _token_mixer(h, positions, w_in[l], b_fgate[l], gn_a[l], gn_b[l], w_out[l])
        x = _layer_norm(DEEPNORM_ALPHA * x + g_a * mix, ln1_g[l], ln1_b[l])
        h = x * (1.0 + sc_f) + sh_f
        ffn = _conv_ffn(h, w_up[l], conv_w[l], conv_b[l], w_down[l])
        x = _layer_norm(DEEPNORM_ALPHA * x + g_f * ffn, ln2_g[l], ln2_b[l])
    return x
```

```python
import functools

import jax
import jax.numpy as jnp
from jax import lax
from jax.experimental import pallas as pl
from jax.experimental.pallas import tpu as pltpu

F32 = jnp.float32
BF16 = jnp.bfloat16

HEAD_DIM = 64
N_HEADS = 8
WIDTH = N_HEADS * HEAD_DIM
LANES = 128
N_PAIRS = WIDTH // LANES
BLOCK = 128
DILATED_PATTERNS = ((128, 1), (512, 4), (2048, 16))
ROPE_THETA = 500000.0
ROPE_DIMS = HEAD_DIM // 4
ROPE_HALF = ROPE_DIMS // 2
CONV_WIDTH = 3
LN_EPS = 1e-5
RMS_EPS = 1e-6
NEG = -1e30
F_PAD = LANES
FFN_CHUNK = 256
HALO = 16
VMEM_LIMIT = 56 * 1024 * 1024


def _cparams(sem):
    return pltpu.CompilerParams(dimension_semantics=sem, vmem_limit_bytes=VMEM_LIMIT)


def _ada_kernel(c_ref, w_ref, b_ref, o_ref):
    c = c_ref[...]
    s = c / (1.0 + jnp.exp(-c))
    o_ref[...] = jnp.dot(s, w_ref[...], preferred_element_type=F32,
                         precision=lax.Precision.HIGHEST) + b_ref[...]


def _ada(c, w, b):
    B, D = c.shape
    N = w.shape[1]
    return pl.pallas_call(
        _ada_kernel,
        grid=(N // D,),
        in_specs=[pl.BlockSpec((B, D), lambda j: (0, 0)),
                  pl.BlockSpec((D, D), lambda j: (0, j)),
                  pl.BlockSpec((1, D), lambda j: (0, j))],
        out_specs=pl.BlockSpec((B, D), lambda j: (0, j)),
        out_shape=jax.ShapeDtypeStruct((B, N), F32),
        compiler_params=_cparams(("arbitrary",)),
        name="ada",
    )(c, w, b.reshape(1, N))


def _trig_kernel(freq_ref, pos_ref, cos_ref, sin_ref):
    p = pos_ref[...].astype(F32)
    for k in range(ROPE_HALF):
        ang = p * freq_ref[k]
        cos_ref[k] = jnp.cos(ang)
        sin_ref[k] = jnp.sin(ang)


def _rotary_tables(positions):
    B, S = positions.shape
    rows = B * S // LANES
    freqs = ROPE_THETA ** (-jnp.arange(0, ROPE_DIMS, 2, dtype=F32) / ROPE_DIMS)
    cos_t, sin_t = pl.pallas_call(
        _trig_kernel,
        in_specs=[pl.BlockSpec(memory_space=pltpu.SMEM),
                  pl.BlockSpec(memory_space=pltpu.VMEM)],
        out_specs=[pl.BlockSpec(memory_space=pltpu.VMEM)] * 2,
        out_shape=[jax.ShapeDtypeStruct((ROPE_HALF, rows, LANES), F32)] * 2,
        compiler_params=pltpu.CompilerParams(vmem_limit_bytes=VMEM_LIMIT),
        name="trig",
    )(freqs, positions.reshape(rows, LANES))

    def widen(t, fill):
        t = t.reshape(ROPE_HALF, B, S).transpose(1, 2, 0)
        rest = jnp.full((B, S, HEAD_DIM - ROPE_DIMS), fill, F32)
        head = jnp.concatenate([t, t, rest], axis=-1)
        return jnp.concatenate([head, head], axis=-1)

    return widen(cos_t, 1.0), widen(sin_t, 0.0)


def _inproj_kernel(x_ref, sc_ref, sh_ref, w_ref, cos_ref, sin_ref,
                   qa_ref, ka_ref, va_ref, qb_ref, kb_ref, vb_ref, f_ref):
    ts = x_ref.shape[0]
    h = (x_ref[...] * (1.0 + sc_ref[...]) + sh_ref[...]).astype(BF16)

    def proj(g, width=WIDTH):
        return jnp.dot(h, w_ref[:, g * WIDTH:g * WIDTH + width], preferred_element_type=F32)

    qa_ref[...] = proj(0).astype(BF16)
    ka_ref[...] = proj(1).astype(BF16)
    va_ref[...] = proj(2).astype(BF16)
    vb_ref[...] = proj(5).astype(BF16)
    f_ref[...] = proj(6, F_PAD)

    cosv = cos_ref[...]
    sinv = sin_ref[...]
    lane = lax.broadcasted_iota(jnp.int32, (ts, LANES), 1)
    first_half = (lane & (HEAD_DIM - 1)) < ROPE_HALF

    def rotary(t):
        nxt = pltpu.roll(t, LANES - ROPE_HALF, 1)
        prv = pltpu.roll(t, ROPE_HALF, 1)
        return t * cosv + jnp.where(first_half, -nxt, prv) * sinv

    for g, ref in ((3, qb_ref), (4, kb_ref)):
        z = proj(g)
        for p in range(N_PAIRS):
            cs = slice(p * LANES, (p + 1) * LANES)
            ref[:, cs] = rotary(z[:, cs]).astype(BF16)


def _inproj(x, sc, sh, w_cat, cos_t, sin_t, ts):
    B, S, D = x.shape
    n_cols = w_cat.shape[1]
    row = lambda b, i: (b, i, 0)
    per_b = lambda b, i: (b, 0, 0)
    wide = pl.BlockSpec((None, ts, WIDTH), row)
    tab = pl.BlockSpec((None, ts, LANES), row)
    return pl.pallas_call(
        _inproj_kernel,
        grid=(B, S // ts),
        in_specs=[pl.BlockSpec((None, ts, D), row),
                  pl.BlockSpec((None, 1, D), per_b),
                  pl.BlockSpec((None, 1, D), per_b),
                  pl.BlockSpec((D, n_cols), lambda b, i: (0, 0)),
                  tab, tab],
        out_specs=[wide] * 6 + [tab],
        out_shape=[jax.ShapeDtypeStruct((B, S, WIDTH), BF16)] * 6
                  + [jax.ShapeDtypeStruct((B, S, F_PAD), F32)],
        compiler_params=_cparams(("parallel", "parallel")),
        name="inproj",
    )(x, sc, sh, w_cat, cos_t, sin_t)


def _fcumsum_kernel(f_ref, b_ref, o_ref):
    S = f_ref.shape[0]
    r = lax.broadcasted_iota(jnp.int32, (BLOCK, BLOCK), 0)
    c = lax.broadcasted_iota(jnp.int32, (BLOCK, BLOCK), 1)
    tri = jnp.where(c <= r, 1.0, 0.0).astype(BF16)

    def body(n, carry):
        rows = pl.ds(pl.multiple_of(n * BLOCK, BLOCK), BLOCK)
        z = f_ref[rows, :] + b_ref[...]
        logf = jnp.minimum(z, 0.0) - jnp.log(1.0 + jnp.exp(-jnp.abs(z)))
        hi = logf.astype(BF16)
        r1 = logf - hi.astype(F32)
        mid = r1.astype(BF16)
        lo = (r1 - mid.astype(F32)).astype(BF16)
        local = (jnp.dot(tri, hi, preferred_element_type=F32)
                 + jnp.dot(tri, mid, preferred_element_type=F32)
                 + jnp.dot(tri, lo, preferred_element_type=F32))
        tot = local + carry
        o_ref[rows, :] = tot
        return tot[BLOCK - 1:BLOCK, :]

    lax.fori_loop(0, S // BLOCK, body, jnp.zeros((1, F_PAD), F32))


def _fcumsum(f_slab, b_pad):
    B, S, _ = f_slab.shape
    blk = pl.BlockSpec((None, S, F_PAD), lambda b: (b, 0, 0))
    return pl.pallas_call(
        _fcumsum_kernel,
        grid=(B,),
        in_specs=[blk, pl.BlockSpec((1, F_PAD), lambda b: (0, 0))],
        out_specs=blk,
        out_shape=jax.ShapeDtypeStruct((B, S, F_PAD), F32),
        compiler_params=_cparams(("parallel",)),
        name="fcumsum",
    )(f_slab, b_pad)


def _head_rms_gain(o, is_h0, gain):
    sq = o * o
    ms0 = jnp.sum(jnp.where(is_h0, sq, 0.0), axis=-1, keepdims=True) * (1.0 / HEAD_DIM)
    ms1 = jnp.sum(jnp.where(is_h0, 0.0, sq), axis=-1, keepdims=True) * (1.0 / HEAD_DIM)
    inv = jnp.where(is_h0, lax.rsqrt(ms0 + RMS_EPS), lax.rsqrt(ms1 + RMS_EPS))
    return o * inv * gain


_NT = (((1,), (1,)), ((), ()))


def _fox_kernel(q_ref, k_ref, v_ref, f_ref, gn_ref, o_ref, *, tq):
    i = pl.program_id(1)
    row = lax.broadcasted_iota(jnp.int32, (tq, tq), 0)
    col = lax.broadcasted_iota(jnp.int32, (tq, tq), 1)
    causal = col <= row
    lane = lax.broadcasted_iota(jnp.int32, (tq, LANES), 1)
    is_h0 = lane < HEAD_DIM

    for p in range(N_PAIRS):
        cs = slice(p * LANES, (p + 1) * LANES)
        qp = q_ref[:, cs]
        zero = jnp.zeros_like(qp)
        qh = (jnp.where(is_h0, qp, zero), jnp.where(is_h0, zero, qp))
        fq = tuple(f_ref[2 * p + hh, i, :, 0:1] for hh in range(2))

        def tile(kb, carry, masked):
            rows = pl.ds(pl.multiple_of(kb * tq, tq), tq)
            k = k_ref[rows, cs]
            v = v_ref[rows, cs]
            new = []
            for hh in range(2):
                m, l, acc = carry[hh]
                s = lax.dot_general(qh[hh], k, _NT, preferred_element_type=F32)
                s = s + (fq[hh] - f_ref[2 * p + hh, kb])
                if masked:
                    s = jnp.where(causal, s, NEG)
                m_new = jnp.maximum(m, jnp.max(s, axis=-1, keepdims=True))
                alpha = jnp.exp(m - m_new)
                pm = jnp.exp(s - m_new)
                l = alpha * l + jnp.sum(pm, axis=-1, keepdims=True)
                acc = alpha * acc + jnp.dot(pm.astype(BF16), v, preferred_element_type=F32)
                new.append((m_new, l, acc))
            return tuple(new)

        init = tuple((jnp.full((tq, 1), NEG, F32), jnp.zeros((tq, 1), F32),
                      jnp.zeros((tq, LANES), F32)) for _ in range(2))
        carry = lax.fori_loop(0, i, lambda kb, c: tile(kb, c, False), init)
        carry = tile(i, carry, True)
        o = jnp.where(is_h0, carry[0][2] / carry[0][1], carry[1][2] / carry[1][1])
        o_ref[:, cs] = _head_rms_gain(o, is_h0, gn_ref[:, cs]).astype(BF16)


def _fox(q, k, v, f_blocks, gn, tq):
    B, S, _ = q.shape
    full = pl.BlockSpec((None, S, WIDTH), lambda b, i: (b, 0, 0))
    return pl.pallas_call(
        functools.partial(_fox_kernel, tq=tq),
        grid=(B, S // tq),
        in_specs=[pl.BlockSpec((None, tq, WIDTH), lambda b, i: (b, i, 0)),
                  full, full,
                  pl.BlockSpec((None, N_HEADS, S // tq, 1, tq), lambda b, i: (b, 0, 0, 0, 0)),
                  pl.BlockSpec((1, WIDTH), lambda b, i: (0, 0))],
        out_specs=pl.BlockSpec((None, tq, WIDTH), lambda b, i: (b, i, 0)),
        out_shape=jax.ShapeDtypeStruct((B, S, WIDTH), BF16),
        compiler_params=_cparams(("parallel", "arbitrary")),
        name="fox",
    )(q, k, v, f_blocks, gn)


def _dil_kernel(q_ref, k_ref, v_ref, gn_ref, o_ref,
                qf, kf, vf, acc_s, m0_s, m1_s, l0_s, l1_s):
    S = q_ref.shape[0]
    qf[...] = q_ref[...].astype(F32)
    kf[...] = k_ref[...].astype(F32)
    vf[...] = v_ref[...].astype(F32)
    lane = lax.broadcasted_iota(jnp.int32, (BLOCK, LANES), 1)
    is_h0 = lane < HEAD_DIM
    m_s = (m0_s, m1_s)
    l_s = (l0_s, l1_s)

    for pi, (window, d) in enumerate(DILATED_PATTERNS):
        steps = window // d
        nb = S // d // BLOCK
        first = pi == 0

        def block(start, kstart, nk, first=first, d=d, steps=steps):
            if d == 1:
                rows_q = pl.ds(start, BLOCK)
                rows_k = pl.ds(kstart, nk)
            else:
                rows_q = pl.ds(start, BLOCK, stride=d)
                rows_k = pl.ds(kstart, nk, stride=d)
            qi = lax.broadcasted_iota(jnp.int32, (BLOCK, nk), 0)
            kj = lax.broadcasted_iota(jnp.int32, (BLOCK, nk), 1)
            dist = qi + (nk - BLOCK) - kj
            mask = (dist >= 0) & (dist <= steps)
            q = qf[rows_q, :].astype(BF16)
            k = kf[rows_k, :].astype(BF16)
            v = vf[rows_k, :].astype(BF16)
            zero = jnp.zeros_like(q)
            pv, alphas = [], []
            for hh in range(2):
                qh = jnp.where(is_h0, q, zero) if hh == 0 else jnp.where(is_h0, zero, q)
                s = lax.dot_general(qh, k, _NT, preferred_element_type=F32)
                s = jnp.where(mask, s, NEG)
                mb = jnp.broadcast_to(jnp.max(s, axis=-1, keepdims=True), (BLOCK, LANES))
                if first:
                    m_new = mb
                else:
                    m_old = m_s[hh][rows_q, :]
                    m_new = jnp.maximum(m_old, mb)
                    alpha = jnp.exp(m_old - m_new)
                    alphas.append(alpha)
                m_wide = m_new if nk == LANES else jnp.concatenate([m_new] * (nk // LANES), axis=1)
                pm = jnp.exp(s - m_wide)
                lsum = jnp.broadcast_to(jnp.sum(pm, axis=-1, keepdims=True), (BLOCK, LANES))
                if not first:
                    lsum = lsum + alpha * l_s[hh][rows_q, :]
                m_s[hh][rows_q, :] = m_new
                l_s[hh][rows_q, :] = lsum
                pv.append(jnp.dot(pm.astype(BF16), v, preferred_element_type=F32))
            acc = jnp.where(is_h0, pv[0], pv[1])
            if not first:
                acc = acc + jnp.where(is_h0, alphas[0], alphas[1]) * acc_s[rows_q, :]
            acc_s[rows_q, :] = acc

        def per_residue(r, _, d=d, nb=nb, block=block):
            block(r, r, BLOCK)

            def later(blk, _):
                base = d * BLOCK * (blk - 1)
                if d == 1:
                    base = pl.multiple_of(base, BLOCK)
                block(r + base + d * BLOCK, r + base, 2 * BLOCK)
                return 0

            lax.fori_loop(1, nb, later, 0)
            return 0

        if d == 1:
            per_residue(0, 0)
        else:
            lax.fori_loop(0, d, per_residue, 0)

    chunk = 2 * BLOCK
    lane_c = lax.broadcasted_iota(jnp.int32, (chunk, LANES), 1)
    is_h0_c = lane_c < HEAD_DIM

    def finish(n, _):
        rows = pl.ds(pl.multiple_of(n * chunk, chunk), chunk)
        o = acc_s[rows, :] / jnp.where(is_h0_c, l0_s[rows, :], l1_s[rows, :])
        o_ref[rows, :] = _head_rms_gain(o, is_h0_c, gn_ref[...]).astype(BF16)
        return 0

    lax.fori_loop(0, S // chunk, finish, 0)


def _dilated(q, k, v, gn):
    B, S, _ = q.shape
    blk = pl.BlockSpec((None, S, LANES), lambda b, p: (b, 0, p))
    return pl.pallas_call(
        _dil_kernel,
        grid=(B, N_PAIRS),
        in_specs=[blk, blk, blk, pl.BlockSpec((1, LANES), lambda b, p: (0, p))],
        out_specs=blk,
        out_shape=jax.ShapeDtypeStruct((B, S, WIDTH), BF16),
        scratch_shapes=[pltpu.VMEM((S, LANES), F32)] * 8,
        compiler_params=_cparams(("parallel", "parallel")),
        name="dilated",
    )(q, k, v, gn)


def _layer_norm(x, g, b):
    mu = jnp.mean(x, axis=-1, keepdims=True)
    xc = x - mu
    var = jnp.mean(xc * xc, axis=-1, keepdims=True)
    return xc * lax.rsqrt(var + LN_EPS) * g + b


def _outproj_kernel(oa_ref, ob_ref, x_ref, wo_ref, ga_ref, lg_ref, lb_ref, sc_ref, sh_ref,
                    x1_ref, h2_ref, *, alpha):
    mix = (jnp.dot(oa_ref[...], wo_ref[0:WIDTH, :], preferred_element_type=F32)
           + jnp.dot(ob_ref[...], wo_ref[WIDTH:2 * WIDTH, :], preferred_element_type=F32))
    x1 = _layer_norm(alpha * x_ref[...] + ga_ref[...] * mix, lg_ref[...], lb_ref[...])
    x1_ref[...] = x1
    h2_ref[...] = (x1 * (1.0 + sc_ref[...]) + sh_ref[...]).astype(BF16)


def _outproj(oa, ob, x, w_out, g_a, ln_g, ln_b, sc_f, sh_f, alpha, ts):
    B, S, D = x.shape
    row = lambda b, i: (b, i, 0)
    per_b = lambda b, i: (b, 0, 0)
    const = lambda b, i: (0, 0)
    vec_b = pl.BlockSpec((None, 1, D), per_b)
    vec = pl.BlockSpec((1, D), const)
    return pl.pallas_call(
        functools.partial(_outproj_kernel, alpha=alpha),
        grid=(B, S // ts),
        in_specs=[pl.BlockSpec((None, ts, WIDTH), row),
                  pl.BlockSpec((None, ts, WIDTH), row),
                  pl.BlockSpec((None, ts, D), row),
                  pl.BlockSpec((2 * WIDTH, D), const),
                  vec_b, vec, vec, vec_b, vec_b],
        out_specs=[pl.BlockSpec((None, ts, D), row)] * 2,
        out_shape=[jax.ShapeDtypeStruct((B, S, D), F32), jax.ShapeDtypeStruct((B, S, D), BF16)],
        compiler_params=_cparams(("parallel", "parallel")),
        name="outproj",
    )(oa, ob, x, w_out, g_a, ln_g, ln_b, sc_f, sh_f)


def _ffn_kernel(h_ref, halo_ref, x1_ref, wup_ref, cw_ref, cb_ref, wdn_ref, gf_ref, lg_ref, lb_ref,
                o_ref, hcat, u_s, act, *, alpha, n_chunks):
    ts = h_ref.shape[0]
    i = pl.program_id(1)
    halo = halo_ref[...]
    hcat[0:HALO, :] = jnp.where(i > 0, halo, jnp.zeros_like(halo))
    hcat[HALO:, :] = h_ref[...]
    hc = hcat[...]
    cc = 2 * FFN_CHUNK
    for c in range(n_chunks):
        cols = slice(c * cc, (c + 1) * cc)
        u_s[...] = jnp.dot(hc, wup_ref[:, cols], preferred_element_type=F32)
        cw = cw_ref[:, cols]
        y = cb_ref[:, cols]
        for t in range(CONV_WIDTH):
            off = HALO - (CONV_WIDTH - 1) + t
            y = y + u_s[off:off + ts, :] * cw[t:t + 1, :]
        a = y[:, :FFN_CHUNK]
        g = y[:, FFN_CHUNK:]
        act[:, c * FFN_CHUNK:(c + 1) * FFN_CHUNK] = (g / (1.0 + jnp.exp(-g)) * a).astype(BF16)
    ffn = jnp.dot(act[...], wdn_ref[...], preferred_element_type=F32)
    o_ref[...] = _layer_norm(alpha * x1_ref[...] + gf_ref[...] * ffn, lg_ref[...], lb_ref[...])


def _ffn(h2, x1, w_up_c, cw_c, cb_c, w_down, g_f, ln_g, ln_b, alpha, ts):
    B, S, D = x1.shape
    d_ff = w_down.shape[0]
    n_chunks = d_ff // FFN_CHUNK
    row = lambda b, i: (b, i, 0)
    const = lambda b, i: (0, 0)
    vec = pl.BlockSpec((1, D), const)
    halo_idx = lambda b, i: (b, jnp.maximum(i * (ts // HALO) - 1, 0), 0)
    return pl.pallas_call(
        functools.partial(_ffn_kernel, alpha=alpha, n_chunks=n_chunks),
        grid=(B, S // ts),
        in_specs=[pl.BlockSpec((None, ts, D), row),
                  pl.BlockSpec((None, HALO, D), halo_idx),
                  pl.BlockSpec((None, ts, D), row),
                  pl.BlockSpec((D, 2 * d_ff), const),
                  pl.BlockSpec((CONV_WIDTH, 2 * d_ff), const),
                  pl.BlockSpec((1, 2 * d_ff), const),
                  pl.BlockSpec((d_ff, D), const),
                  pl.BlockSpec((None, 1, D), lambda b, i: (b, 0, 0)),
                  vec, vec],
        out_specs=pl.BlockSpec((None, ts, D), row),
        out_shape=jax.ShapeDtypeStruct((B, S, D), F32),
        scratch_shapes=[pltpu.VMEM((HALO + ts, D), BF16),
                        pltpu.VMEM((HALO + ts, 2 * FFN_CHUNK), F32),
                        pltpu.VMEM((ts, d_ff), BF16)],
        compiler_params=_cparams(("parallel", "parallel")),
        name="ffn",
    )(h2, h2, x1, w_up_c, cw_c, cb_c, w_down, g_f, ln_g, ln_b)


def _interleave_chunks(t, d_ff):
    lead = t.shape[:-1]
    t = t.reshape(lead + (2, d_ff // FFN_CHUNK, FFN_CHUNK))
    t = jnp.swapaxes(t, -3, -2)
    return t.reshape(lead + (2 * d_ff,))


def kernel(x, c, positions, w_ada, b_ada, w_in, b_fgate, gn_a, gn_b, w_out,
           ln1_g, ln1_b, w_up, conv_w, conv_b, w_down, ln2_g, ln2_b):
    B, S, D = x.shape
    depth = w_ada.shape[0]
    d_ff = w_down.shape[1]
    alpha = (2.0 * depth) ** 0.25
    scale = HEAD_DIM ** -0.5
    ts = 512
    tq = 256

    cos_t, sin_t = _rotary_tables(positions)
    for l in range(depth):
        ada = _ada(c, w_ada[l], b_ada[l])
        sh_a, sc_a, g_a, sh_f, sc_f, g_f = (t[:, None, :] for t in jnp.split(ada, 6, axis=-1))

        w = w_in[l]
        o3 = 3 * WIDTH
        w_f = jnp.pad(w[:, o3:o3 + N_HEADS], ((0, 0), (0, F_PAD - N_HEADS)))
        ob = o3 + N_HEADS
        w_cat = jnp.concatenate(
            [w[:, 0:WIDTH] * scale, w[:, WIDTH:o3],
             w[:, ob:ob + WIDTH] * scale, w[:, ob + WIDTH:ob + 3 * WIDTH], w_f], axis=1).astype(BF16)

        qa, ka, va, qb, kb, vb, f_slab = _inproj(x, sc_a, sh_a, w_cat, cos_t, sin_t, ts)

        b_pad = jnp.pad(b_fgate[l], (0, F_PAD - N_HEADS)).reshape(1, F_PAD)
        f_cum = _fcumsum(f_slab, b_pad)
        f_blocks = f_cum[:, :, :N_HEADS].transpose(0, 2, 1).reshape(B, N_HEADS, S // tq, 1, tq)

        oa = _fox(qa, ka, va, f_blocks, gn_a[l].reshape(1, WIDTH), tq)
        ob_ = _dilated(qb, kb, vb, gn_b[l].reshape(1, WIDTH))

        x1, h2 = _outproj(oa, ob_, x, w_out[l].astype(BF16), g_a,
                          ln1_g[l].reshape(1, D), ln1_b[l].reshape(1, D), sc_f, sh_f, alpha, ts)

        x = _ffn(h2, x1,
                 _interleave_chunks(w_up[l], d_ff).astype(BF16),
                 _interleave_chunks(conv_w[l], d_ff),
                 _interleave_chunks(conv_b[l], d_ff).reshape(1, 2 * d_ff),
                 w_down[l].astype(BF16), g_f,
                 ln2_g[l].reshape(1, D), ln2_b[l].reshape(1, D), alpha, ts)
    return x
```

```python
import functools

import jax
import jax.numpy as jnp
from jax import lax
from jax.experimental import pallas as pl
from jax.experimental.pallas import tpu as pltpu

F32 = jnp.float32
BF16 = jnp.bfloat16

HEAD_DIM = 64
N_HEADS = 8
WIDTH = N_HEADS * HEAD_DIM
LANES = 128
N_PAIRS = WIDTH // LANES
BLOCK = 128
DILATED_PATTERNS = ((128, 1), (512, 4), (2048, 16))
ROPE_THETA = 500000.0
ROPE_DIMS = HEAD_DIM // 4
ROPE_HALF = ROPE_DIMS // 2
CONV_WIDTH = 3
LN_EPS = 1e-5
RMS_EPS = 1e-6
NEG = -1e30
LOG2E = 1.4426950408889634
F_PAD = LANES
FFN_CHUNK = 256
HALO = 16
VMEM_LIMIT = 56 * 1024 * 1024


def _cparams(sem):
    return pltpu.CompilerParams(dimension_semantics=sem, vmem_limit_bytes=VMEM_LIMIT)


def _ada_kernel(c_ref, w_ref, b_ref, o_ref):
    c = c_ref[...]
    s = c / (1.0 + jnp.exp(-c))
    o_ref[...] = jnp.dot(s, w_ref[...], preferred_element_type=F32,
                         precision=lax.Precision.HIGHEST) + b_ref[...]


def _ada(c, w, b):
    B, D = c.shape
    N = w.shape[1]
    return pl.pallas_call(
        _ada_kernel,
        grid=(N // D,),
        in_specs=[pl.BlockSpec((B, D), lambda j: (0, 0)),
                  pl.BlockSpec((D, D), lambda j: (0, j)),
                  pl.BlockSpec((1, D), lambda j: (0, j))],
        out_specs=pl.BlockSpec((B, D), lambda j: (0, j)),
        out_shape=jax.ShapeDtypeStruct((B, N), F32),
        compiler_params=_cparams(("arbitrary",)),
        name="ada",
    )(c, w, b.reshape(1, N))


def _trig_kernel(freq_ref, pos_ref, cos_ref, sin_ref):
    p = pos_ref[...].astype(F32)
    for k in range(ROPE_HALF):
        ang = p * freq_ref[k]
        cos_ref[k] = jnp.cos(ang)
        sin_ref[k] = jnp.sin(ang)


def _rotary_tables(positions):
    B, S = positions.shape
    rows = B * S // LANES
    freqs = ROPE_THETA ** (-jnp.arange(0, ROPE_DIMS, 2, dtype=F32) / ROPE_DIMS)
    cos_t, sin_t = pl.pallas_call(
        _trig_kernel,
        in_specs=[pl.BlockSpec(memory_space=pltpu.SMEM),
                  pl.BlockSpec(memory_space=pltpu.VMEM)],
        out_specs=[pl.BlockSpec(memory_space=pltpu.VMEM)] * 2,
        out_shape=[jax.ShapeDtypeStruct((ROPE_HALF, rows, LANES), F32)] * 2,
        compiler_params=pltpu.CompilerParams(vmem_limit_bytes=VMEM_LIMIT),
        name="trig",
    )(freqs, positions.reshape(rows, LANES))

    def widen(t, fill):
        t = t.reshape(ROPE_HALF, B, S).transpose(1, 2, 0)
        rest = jnp.full((B, S, HEAD_DIM - ROPE_DIMS), fill, F32)
        head = jnp.concatenate([t, t, rest], axis=-1)
        return jnp.concatenate([head, head], axis=-1)

    return widen(cos_t, 1.0), widen(sin_t, 0.0)


def _inproj_kernel(x_ref, sc_ref, sh_ref, w_ref, cos_ref, sin_ref,
                   qa_ref, ka_ref, va_ref, qb_ref, kb_ref, vb_ref, f_ref):
    ts = x_ref.shape[0]
    h = (x_ref[...] * (1.0 + sc_ref[...]) + sh_ref[...]).astype(BF16)

    def proj(g, width=WIDTH):
        return jnp.dot(h, w_ref[:, g * WIDTH:g * WIDTH + width], preferred_element_type=F32)

    qa_ref[...] = (proj(0) * LOG2E).astype(BF16)
    ka_ref[...] = proj(1).astype(BF16)
    va_ref[...] = proj(2).astype(BF16)
    vb_ref[...] = proj(5).astype(BF16)
    f_ref[...] = proj(6, F_PAD)

    cosv = cos_ref[...]
    sinv = sin_ref[...]
    lane = lax.broadcasted_iota(jnp.int32, (ts, LANES), 1)
    first_half = (lane & (HEAD_DIM - 1)) < ROPE_HALF

    def rotary(t):
        nxt = pltpu.roll(t, LANES - ROPE_HALF, 1)
        prv = pltpu.roll(t, ROPE_HALF, 1)
        return t * cosv + jnp.where(first_half, -nxt, prv) * sinv

    for g, ref in ((3, qb_ref), (4, kb_ref)):
        z = proj(g)
        for p in range(N_PAIRS):
            cs = slice(p * LANES, (p + 1) * LANES)
            ref[:, cs] = rotary(z[:, cs]).astype(BF16)


def _inproj(x, sc, sh, w_cat, cos_t, sin_t, ts):
    B, S, D = x.shape
    n_cols = w_cat.shape[1]
    row = lambda b, i: (b, i, 0)
    per_b = lambda b, i: (b, 0, 0)
    wide = pl.BlockSpec((None, ts, WIDTH), row)
    tab = pl.BlockSpec((None, ts, LANES), row)
    return pl.pallas_call(
        _inproj_kernel,
        grid=(B, S // ts),
        in_specs=[pl.BlockSpec((None, ts, D), row),
                  pl.BlockSpec((None, 1, D), per_b),
                  pl.BlockSpec((None, 1, D), per_b),
                  pl.BlockSpec((D, n_cols), lambda b, i: (0, 0)),
                  tab, tab],
        out_specs=[wide] * 6 + [tab],
        out_shape=[jax.ShapeDtypeStruct((B, S, WIDTH), BF16)] * 6
                  + [jax.ShapeDtypeStruct((B, S, F_PAD), F32)],
        compiler_params=_cparams(("parallel", "parallel")),
        name="inproj",
    )(x, sc, sh, w_cat, cos_t, sin_t)


def _fcumsum_kernel(f_ref, b_ref, o_ref):
    S = f_ref.shape[0]
    r = lax.broadcasted_iota(jnp.int32, (BLOCK, BLOCK), 0)
    c = lax.broadcasted_iota(jnp.int32, (BLOCK, BLOCK), 1)
    tri = jnp.where(c <= r, 1.0, 0.0).astype(BF16)

    def body(n, carry):
        rows = pl.ds(pl.multiple_of(n * BLOCK, BLOCK), BLOCK)
        z = f_ref[rows, :] + b_ref[...]
        logf = jnp.minimum(z, 0.0) - jnp.log(1.0 + jnp.exp(-jnp.abs(z)))
        hi = logf.astype(BF16)
        r1 = logf - hi.astype(F32)
        mid = r1.astype(BF16)
        lo = (r1 - mid.astype(F32)).astype(BF16)
        local = (jnp.dot(tri, hi, preferred_element_type=F32)
                 + jnp.dot(tri, mid, preferred_element_type=F32)
                 + jnp.dot(tri, lo, preferred_element_type=F32))
        tot = local + carry
        o_ref[rows, :] = tot * LOG2E
        return tot[BLOCK - 1:BLOCK, :]

    lax.fori_loop(0, S // BLOCK, body, jnp.zeros((1, F_PAD), F32))


def _fcumsum(f_slab, b_pad):
    B, S, _ = f_slab.shape
    blk = pl.BlockSpec((None, S, F_PAD), lambda b: (b, 0, 0))
    return pl.pallas_call(
        _fcumsum_kernel,
        grid=(B,),
        in_specs=[blk, pl.BlockSpec((1, F_PAD), lambda b: (0, 0))],
        out_specs=blk,
        out_shape=jax.ShapeDtypeStruct((B, S, F_PAD), F32),
        compiler_params=_cparams(("parallel",)),
        name="fcumsum",
    )(f_slab, b_pad)


def _head_rms_gain(o, is_h0, gain):
    sq = o * o
    ms0 = jnp.sum(jnp.where(is_h0, sq, 0.0), axis=-1, keepdims=True) * (1.0 / HEAD_DIM)
    ms1 = jnp.sum(jnp.where(is_h0, 0.0, sq), axis=-1, keepdims=True) * (1.0 / HEAD_DIM)
    inv = jnp.where(is_h0, lax.rsqrt(ms0 + RMS_EPS), lax.rsqrt(ms1 + RMS_EPS))
    return o * inv * gain


_NT = (((1,), (1,)), ((), ()))


def _fox_kernel(q_ref, k_ref, v_ref, f_ref, gn_ref, o_ref, q_s, v_s, m_s, acc_s, *, tq):
    i = pl.program_id(1)
    S = k_ref.shape[0]
    lane = lax.broadcasted_iota(jnp.int32, (tq, LANES), 1)
    is_h0 = lane < HEAD_DIM
    own = (is_h0, jnp.logical_not(is_h0))
    den_lane = (HEAD_DIM, 0)

    @pl.when(i == 0)
    def _():
        def fill(n, _):
            rows = pl.ds(pl.multiple_of(n * tq, tq), tq)
            for p in range(N_PAIRS):
                vp = v_ref[rows, p * LANES:(p + 1) * LANES].astype(F32)
                for hh in range(2):
                    h = 2 * p + hh
                    ones = jnp.where(lane == den_lane[hh], 1.0, 0.0)
                    v_s[rows, h * LANES:(h + 1) * LANES] = jnp.where(own[hh], vp, ones).astype(BF16)
            return 0
        lax.fori_loop(0, S // tq, fill, 0)

    for p in range(N_PAIRS):
        qp = q_ref[:, p * LANES:(p + 1) * LANES].astype(F32)
        for hh in range(2):
            h = 2 * p + hh
            q_s[:, h * LANES:(h + 1) * LANES] = jnp.where(own[hh], qp, 0.0).astype(BF16)
    m_s[...] = jnp.full(m_s.shape, NEG, F32)
    acc_s[...] = jnp.zeros(acc_s.shape, F32)

    row = lax.broadcasted_iota(jnp.int32, (tq, tq), 0)
    col = lax.broadcasted_iota(jnp.int32, (tq, tq), 1)
    causal = col <= row
    fq = tuple(f_ref[h, i, :, 0:1] for h in range(N_HEADS))

    def tile(kb, masked):
        rows = pl.ds(pl.multiple_of(kb * tq, tq), tq)
        for h in range(N_HEADS):
            hs = slice(h * LANES, (h + 1) * LANES)
            p = h // 2
            k = k_ref[rows, p * LANES:(p + 1) * LANES]
            s = lax.dot_general(q_s[:, hs], k, _NT, preferred_element_type=F32)
            s = s + (fq[h] - f_ref[h, kb])
            if masked:
                s = jnp.where(causal, s, NEG)
            m_old = m_s[h]
            m_new = jnp.maximum(m_old, jnp.max(s, axis=-1, keepdims=True))
            alpha = jnp.exp2(m_old - m_new)
            pm = jnp.exp2(s - jnp.concatenate([m_new] * (tq // LANES), axis=1))
            pv = jnp.dot(pm.astype(BF16), v_s[rows, hs], preferred_element_type=F32)
            acc_s[h] = alpha * acc_s[h] + pv
            m_s[h] = m_new

    def body(kb, _):
        tile(kb, False)
        return 0

    lax.fori_loop(0, i, body, 0)
    tile(i, True)

    for p in range(N_PAIRS):
        o = []
        for hh in range(2):
            acc = acc_s[2 * p + hh]
            den = jnp.sum(jnp.where(lane == den_lane[hh], acc, 0.0), axis=-1, keepdims=True)
            o.append(acc / den)
        o = jnp.where(is_h0, o[0], o[1])
        cs = slice(p * LANES, (p + 1) * LANES)
        o_ref[:, cs] = _head_rms_gain(o, is_h0, gn_ref[:, cs]).astype(BF16)


def _fox(q, k, v, f_blocks, gn, tq):
    B, S, _ = q.shape
    full = pl.BlockSpec((None, S, WIDTH), lambda b, i: (b, 0, 0))
    return pl.pallas_call(
        functools.partial(_fox_kernel, tq=tq),
        grid=(B, S // tq),
        in_specs=[pl.BlockSpec((None, tq, WIDTH), lambda b, i: (b, i, 0)),
                  full, full,
                  pl.BlockSpec((None, N_HEADS, S // tq, 1, tq), lambda b, i: (b, 0, 0, 0, 0)),
                  pl.BlockSpec((1, WIDTH), lambda b, i: (0, 0))],
        out_specs=pl.BlockSpec((None, tq, WIDTH), lambda b, i: (b, i, 0)),
        out_shape=jax.ShapeDtypeStruct((B, S, WIDTH), BF16),
        scratch_shapes=[pltpu.VMEM((tq, N_HEADS * LANES), BF16),
                        pltpu.VMEM((S, N_HEADS * LANES), BF16),
                        pltpu.VMEM((N_HEADS, tq, LANES), F32),
                        pltpu.VMEM((N_HEADS, tq, LANES), F32)],
        compiler_params=_cparams(("parallel", "arbitrary")),
        name="fox",
    )(q, k, v, f_blocks, gn)


def _dil_kernel(q_ref, k_ref, v_ref, gn_ref, o_ref,
                qf, kf, vf, acc_s, m0_s, m1_s, l0_s, l1_s):
    S = q_ref.shape[0]
    qf[...] = q_ref[...].astype(F32)
    kf[...] = k_ref[...].astype(F32)
    vf[...] = v_ref[...].astype(F32)
    lane = lax.broadcasted_iota(jnp.int32, (BLOCK, LANES), 1)
    is_h0 = lane < HEAD_DIM
    m_s = (m0_s, m1_s)
    l_s = (l0_s, l1_s)

    for pi, (window, d) in enumerate(DILATED_PATTERNS):
        steps = window // d
        nb = S // d // BLOCK
        first = pi == 0

        def block(start, kstart, nk, first=first, d=d, steps=steps):
            if d == 1:
                rows_q = pl.ds(start, BLOCK)
                rows_k = pl.ds(kstart, nk)
            else:
                rows_q = pl.ds(start, BLOCK, stride=d)
                rows_k = pl.ds(kstart, nk, stride=d)
            qi = lax.broadcasted_iota(jnp.int32, (BLOCK, nk), 0)
            kj = lax.broadcasted_iota(jnp.int32, (BLOCK, nk), 1)
            dist = qi + (nk - BLOCK) - kj
            mask = (dist >= 0) & (dist <= steps)
            q = qf[rows_q, :].astype(BF16)
            k = kf[rows_k, :].astype(BF16)
            v = vf[rows_k, :].astype(BF16)
            zero = jnp.zeros_like(q)
            pv, alphas = [], []
            for hh in range(2):
                qh = jnp.where(is_h0, q, zero) if hh == 0 else jnp.where(is_h0, zero, q)
                s = lax.dot_general(qh, k, _NT, preferred_element_type=F32)
                s = jnp.where(mask, s, NEG)
                mb = jnp.broadcast_to(jnp.max(s, axis=-1, keepdims=True), (BLOCK, LANES))
                if first:
                    m_new = mb
                else:
                    m_old = m_s[hh][rows_q, :]
                    m_new = jnp.maximum(m_old, mb)
                    alpha = jnp.exp(m_old - m_new)
                    alphas.append(alpha)
                m_wide = m_new if nk == LANES else jnp.concatenate([m_new] * (nk // LANES), axis=1)
                pm = jnp.exp(s - m_wide)
                lsum = jnp.broadcast_to(jnp.sum(pm, axis=-1, keepdims=True), (BLOCK, LANES))
                if not first:
                    lsum = lsum + alpha * l_s[hh][rows_q, :]
                m_s[hh][rows_q, :] = m_new
                l_s[hh][rows_q, :] = lsum
                pv.append(jnp.dot(pm.astype(BF16), v, preferred_element_type=F32))
            acc = jnp.where(is_h0, pv[0], pv[1])
            if not first:
                acc = acc + jnp.where(is_h0, alphas[0], alphas[1]) * acc_s[rows_q, :]
            acc_s[rows_q, :] = acc

        def per_residue(r, _, d=d, nb=nb, block=block):
            block(r, r, BLOCK)

            def later(blk, _):
                base = d * BLOCK * (blk - 1)
                if d == 1:
                    base = pl.multiple_of(base, BLOCK)
                block(r + base + d * BLOCK, r + base, 2 * BLOCK)
                return 0

            lax.fori_loop(1, nb, later, 0)
            return 0

        if d == 1:
            per_residue(0, 0)
        else:
            lax.fori_loop(0, d, per_residue, 0)

    chunk = 2 * BLOCK
    lane_c = lax.broadcasted_iota(jnp.int32, (chunk, LANES), 1)
    is_h0_c = lane_c < HEAD_DIM

    def finish(n, _):
        rows = pl.ds(pl.multiple_of(n * chunk, chunk), chunk)
        o = acc_s[rows, :] / jnp.where(is_h0_c, l0_s[rows, :], l1_s[rows, :])
        o_ref[rows, :] = _head_rms_gain(o, is_h0_c, gn_ref[...]).astype(BF16)
        return 0

    lax.fori_loop(0, S // chunk, finish, 0)


def _dilated(q, k, v, gn):
    B, S, _ = q.shape
    blk = pl.BlockSpec((None, S, LANES), lambda b, p: (b, 0, p))
    return pl.pallas_call(
        _dil_kernel,
        grid=(B, N_PAIRS),
        in_specs=[blk, blk, blk, pl.BlockSpec((1, LANES), lambda b, p: (0, p))],
        out_specs=blk,
        out_shape=jax.ShapeDtypeStruct((B, S, WIDTH), BF16),
        scratch_shapes=[pltpu.VMEM((S, LANES), F32)] * 8,
        compiler_params=_cparams(("parallel", "parallel")),
        name="dilated",
    )(q, k, v, gn)


def _layer_norm(x, g, b):
    mu = jnp.mean(x, axis=-1, keepdims=True)
    xc = x - mu
    var = jnp.mean(xc * xc, axis=-1, keepdims=True)
    return xc * lax.rsqrt(var + LN_EPS) * g + b


def _outproj_kernel(oa_ref, ob_ref, x_ref, wo_ref, ga_ref, lg_ref, lb_ref, sc_ref, sh_ref,
                    x1_ref, h2_ref, *, alpha):
    mix = (jnp.dot(oa_ref[...], wo_ref[0:WIDTH, :], preferred_element_type=F32)
           + jnp.dot(ob_ref[...], wo_ref[WIDTH:2 * WIDTH, :], preferred_element_type=F32))
    x1 = _layer_norm(alpha * x_ref[...] + ga_ref[...] * mix, lg_ref[...], lb_ref[...])
    x1_ref[...] = x1
    h2_ref[...] = (x1 * (1.0 + sc_ref[...]) + sh_ref[...]).astype(BF16)


def _outproj(oa, ob, x, w_out, g_a, ln_g, ln_b, sc_f, sh_f, alpha, ts):
    B, S, D = x.shape
    row = lambda b, i: (b, i, 0)
    per_b = lambda b, i: (b, 0, 0)
    const = lambda b, i: (0, 0)
    vec_b = pl.BlockSpec((None, 1, D), per_b)
    vec = pl.BlockSpec((1, D), const)
    return pl.pallas_call(
        functools.partial(_outproj_kernel, alpha=alpha),
        grid=(B, S // ts),
        in_specs=[pl.BlockSpec((None, ts, WIDTH), row),
                  pl.BlockSpec((None, ts, WIDTH), row),
                  pl.BlockSpec((None, ts, D), row),
                  pl.BlockSpec((2 * WIDTH, D), const),
                  vec_b, vec, vec, vec_b, vec_b],
        out_specs=[pl.BlockSpec((None, ts, D), row)] * 2,
        out_shape=[jax.ShapeDtypeStruct((B, S, D), F32), jax.ShapeDtypeStruct((B, S, D), BF16)],
        compiler_params=_cparams(("parallel", "parallel")),
        name="outproj",
    )(oa, ob, x, w_out, g_a, ln_g, ln_b, sc_f, sh_f)


def _ffn_kernel(h_ref, halo_ref, x1_ref, wup_ref, cw_ref, cb_ref, wdn_ref, gf_ref, lg_ref, lb_ref,
                o_ref, hcat, u_s, act, *, alpha, n_chunks):
    ts = h_ref.shape[0]
    i = pl.program_id(1)
    halo = halo_ref[...]
    hcat[0:HALO, :] = jnp.where(i > 0, halo, jnp.zeros_like(halo))
    hcat[HALO:, :] = h_ref[...]
    hc = hcat[...]
    cc = 2 * FFN_CHUNK
    for c in range(n_chunks):
        cols = slice(c * cc, (c + 1) * cc)
        u_s[...] = jnp.dot(hc, wup_ref[:, cols], preferred_element_type=F32)
        cw = cw_ref[:, cols]
        y = cb_ref[:, cols]
        for t in range(CONV_WIDTH):
            off = HALO - (CONV_WIDTH - 1) + t
            y = y + u_s[off:off + ts, :] * cw[t:t + 1, :]
        a = y[:, :FFN_CHUNK]
        g = y[:, FFN_CHUNK:]
        act[:, c * FFN_CHUNK:(c + 1) * FFN_CHUNK] = (g / (1.0 + jnp.exp(-g)) * a).astype(BF16)
    ffn = jnp.dot(act[...], wdn_ref[...], preferred_element_type=F32)
    o_ref[...] = _layer_norm(alpha * x1_ref[...] + gf_ref[...] * ffn, lg_ref[...], lb_ref[...])


def _ffn(h2, x1, w_up_c, cw_c, cb_c, w_down, g_f, ln_g, ln_b, alpha, ts):
    B, S, D = x1.shape
    d_ff = w_down.shape[0]
    n_chunks = d_ff // FFN_CHUNK
    row = lambda b, i: (b, i, 0)
    const = lambda b, i: (0, 0)
    vec = pl.BlockSpec((1, D), const)
    halo_idx = lambda b, i: (b, jnp.maximum(i * (ts // HALO) - 1, 0), 0)
    return pl.pallas_call(
        functools.partial(_ffn_kernel, alpha=alpha, n_chunks=n_chunks),
        grid=(B, S // ts),
        in_specs=[pl.BlockSpec((None, ts, D), row),
                  pl.BlockSpec((None, HALO, D), halo_idx),
                  pl.BlockSpec((None, ts, D), row),
                  pl.BlockSpec((D, 2 * d_ff), const),
                  pl.BlockSpec((CONV_WIDTH, 2 * d_ff), const),
                  pl.BlockSpec((1, 2 * d_ff), const),
                  pl.BlockSpec((d_ff, D), const),
                  pl.BlockSpec((None, 1, D), lambda b, i: (b, 0, 0)),
                  vec, vec],
        out_specs=pl.BlockSpec((None, ts, D), row),
        out_shape=jax.ShapeDtypeStruct((B, S, D), F32),
        scratch_shapes=[pltpu.VMEM((HALO + ts, D), BF16),
                        pltpu.VMEM((HALO + ts, 2 * FFN_CHUNK), F32),
                        pltpu.VMEM((ts, d_ff), BF16)],
        compiler_params=_cparams(("parallel", "parallel")),
        name="ffn",
    )(h2, h2, x1, w_up_c, cw_c, cb_c, w_down, g_f, ln_g, ln_b)


def _interleave_chunks(t, d_ff):
    lead = t.shape[:-1]
    t = t.reshape(lead + (2, d_ff // FFN_CHUNK, FFN_CHUNK))
    t = jnp.swapaxes(t, -3, -2)
    return t.reshape(lead + (2 * d_ff,))


def kernel(x, c, positions, w_ada, b_ada, w_in, b_fgate, gn_a, gn_b, w_out,
           ln1_g, ln1_b, w_up, conv_w, conv_b, w_down, ln2_g, ln2_b):
    B, S, D = x.shape
    depth = w_ada.shape[0]
    d_ff = w_down.shape[1]
    alpha = (2.0 * depth) ** 0.25
    scale = HEAD_DIM ** -0.5
    ts = 512
    tq = 256

    cos_t, sin_t = _rotary_tables(positions)
    for l in range(depth):
        ada = _ada(c, w_ada[l], b_ada[l])
        sh_a, sc_a, g_a, sh_f, sc_f, g_f = (t[:, None, :] for t in jnp.split(ada, 6, axis=-1))

        w = w_in[l]
        o3 = 3 * WIDTH
        w_f = jnp.pad(w[:, o3:o3 + N_HEADS], ((0, 0), (0, F_PAD - N_HEADS)))
        ob = o3 + N_HEADS
        w_cat = jnp.concatenate(
            [w[:, 0:WIDTH] * scale, w[:, WIDTH:o3],
             w[:, ob:ob + WIDTH] * scale, w[:, ob + WIDTH:ob + 3 * WIDTH], w_f], axis=1).astype(BF16)

        qa, ka, va, qb, kb, vb, f_slab = _inproj(x, sc_a, sh_a, w_cat, cos_t, sin_t, ts)

        b_pad = jnp.pad(b_fgate[l], (0, F_PAD - N_HEADS)).reshape(1, F_PAD)
        f_cum = _fcumsum(f_slab, b_pad)
        f_blocks = f_cum[:, :, :N_HEADS].transpose(0, 2, 1).reshape(B, N_HEADS, S // tq, 1, tq)

        oa = _fox(qa, ka, va, f_blocks, gn_a[l].reshape(1, WIDTH), tq)
        ob_ = _dilated(qb, kb, vb, gn_b[l].reshape(1, WIDTH))

        x1, h2 = _outproj(oa, ob_, x, w_out[l].astype(BF16), g_a,
                          ln1_g[l].reshape(1, D), ln1_b[l].reshape(1, D), sc_f, sh_f, alpha, ts)

        x = _ffn(h2, x1,
                 _interleave_chunks(w_up[l], d_ff).astype(BF16),
                 _interleave_chunks(conv_w[l], d_ff),
                 _interleave_chunks(conv_b[l], d_ff).reshape(1, 2 * d_ff),
                 w_down[l].astype(BF16), g_f,
                 ln2_g[l].reshape(1, D), ln2_b[l].reshape(1, D), alpha, ts)
    return x
```

```python
import functools

import jax
import jax.numpy as jnp
from jax import lax
from jax.experimental import pallas as pl
from jax.experimental.pallas import tpu as pltpu

F32 = jnp.float32
BF16 = jnp.bfloat16

HEAD_DIM = 64
N_HEADS = 8
WIDTH = N_HEADS * HEAD_DIM
LANES = 128
N_PAIRS = WIDTH // LANES
BLOCK = 128
DILATED_PATTERNS = ((128, 1), (512, 4), (2048, 16))
ROPE_THETA = 500000.0
ROPE_DIMS = HEAD_DIM // 4
ROPE_HALF = ROPE_DIMS // 2
CONV_WIDTH = 3
LN_EPS = 1e-5
RMS_EPS = 1e-6
NEG = -1e30
LOG2E = 1.4426950408889634
F_PAD = LANES
FFN_CHUNK = 256
HALO = 16
VMEM_LIMIT = 56 * 1024 * 1024


def _cparams(sem):
    return pltpu.CompilerParams(dimension_semantics=sem, vmem_limit_bytes=VMEM_LIMIT)


def _ada_kernel(c_ref, w_ref, b_ref, o_ref):
    c = c_ref[...]
    s = c / (1.0 + jnp.exp(-c))
    o_ref[...] = jnp.dot(s, w_ref[...], preferred_element_type=F32,
                         precision=lax.Precision.HIGHEST) + b_ref[...]


def _ada(c, w, b):
    B, D = c.shape
    N = w.shape[1]
    return pl.pallas_call(
        _ada_kernel,
        grid=(N // D,),
        in_specs=[pl.BlockSpec((B, D), lambda j: (0, 0)),
                  pl.BlockSpec((D, D), lambda j: (0, j)),
                  pl.BlockSpec((1, D), lambda j: (0, j))],
        out_specs=pl.BlockSpec((B, D), lambda j: (0, j)),
        out_shape=jax.ShapeDtypeStruct((B, N), F32),
        compiler_params=_cparams(("arbitrary",)),
        name="ada",
    )(c, w, b.reshape(1, N))


def _inproj_kernel(x_ref, sc_ref, sh_ref, w_ref, pos_ref, freq_ref,
                   qa_ref, ka_ref, va_ref, qb_ref, kb_ref, vb_ref, f_ref):
    ts = x_ref.shape[0]
    h = (x_ref[...] * (1.0 + sc_ref[...]) + sh_ref[...]).astype(BF16)

    def proj(g, width=WIDTH):
        return jnp.dot(h, w_ref[:, g * WIDTH:g * WIDTH + width], preferred_element_type=F32)

    qa_ref[...] = (proj(0) * LOG2E).astype(BF16)
    ka_ref[...] = proj(1).astype(BF16)
    va_ref[...] = proj(2).astype(BF16)
    vb_ref[...] = proj(5).astype(BF16)
    f_ref[...] = proj(6, F_PAD)

    n_sub = ts // LANES
    p_rows = pos_ref[...]
    p_cols = jnp.concatenate([p_rows, jnp.zeros((8 - n_sub, LANES), F32)], axis=0).T
    ang = jnp.concatenate([p_cols[:, j:j + 1] * freq_ref[...] for j in range(n_sub)], axis=0)
    cosv = jnp.cos(ang)
    sinv = jnp.sin(ang)
    lane = lax.broadcasted_iota(jnp.int32, (ts, LANES), 1)
    first_half = (lane & (HEAD_DIM - 1)) < ROPE_HALF

    def rotary(t):
        nxt = pltpu.roll(t, LANES - ROPE_HALF, 1)
        prv = pltpu.roll(t, ROPE_HALF, 1)
        return t * cosv + jnp.where(first_half, -nxt, prv) * sinv

    for g, ref, gain in ((3, qb_ref, LOG2E), (4, kb_ref, None)):
        z = proj(g)
        for p in range(N_PAIRS):
            cs = slice(p * LANES, (p + 1) * LANES)
            r = rotary(z[:, cs])
            ref[:, cs] = (r if gain is None else r * gain).astype(BF16)


def _inproj(x, sc, sh, w_cat, positions, ts):
    B, S, D = x.shape
    n_cols = w_cat.shape[1]
    row = lambda b, i: (b, i, 0)
    per_b = lambda b, i: (b, 0, 0)
    wide = pl.BlockSpec((None, ts, WIDTH), row)
    tab = pl.BlockSpec((None, ts, LANES), row)
    pos = positions.astype(F32).reshape(B, S // ts, ts // LANES, LANES)
    freqs = ROPE_THETA ** (-jnp.arange(0, ROPE_DIMS, 2, dtype=F32) / ROPE_DIMS)
    head = jnp.concatenate([freqs, freqs, jnp.zeros((HEAD_DIM - ROPE_DIMS,), F32)])
    lane_freq = jnp.concatenate([head, head]).reshape(1, LANES)
    return pl.pallas_call(
        _inproj_kernel,
        grid=(B, S // ts),
        in_specs=[pl.BlockSpec((None, ts, D), row),
                  pl.BlockSpec((None, 1, D), per_b),
                  pl.BlockSpec((None, 1, D), per_b),
                  pl.BlockSpec((D, n_cols), lambda b, i: (0, 0)),
                  pl.BlockSpec((None, None, ts // LANES, LANES), lambda b, i: (b, i, 0, 0)),
                  pl.BlockSpec((1, LANES), lambda b, i: (0, 0))],
        out_specs=[wide] * 6 + [tab],
        out_shape=[jax.ShapeDtypeStruct((B, S, WIDTH), BF16)] * 6
                  + [jax.ShapeDtypeStruct((B, S, F_PAD), F32)],
        compiler_params=_cparams(("parallel", "parallel")),
        name="inproj",
    )(x, sc, sh, w_cat, pos, lane_freq)


def _fcumsum_kernel(f_ref, b_ref, o_ref):
    S = f_ref.shape[0]
    r = lax.broadcasted_iota(jnp.int32, (BLOCK, BLOCK), 0)
    c = lax.broadcasted_iota(jnp.int32, (BLOCK, BLOCK), 1)
    tri = jnp.where(c <= r, 1.0, 0.0).astype(BF16)

    def body(n, carry):
        rows = pl.ds(pl.multiple_of(n * BLOCK, BLOCK), BLOCK)
        z = f_ref[rows, :] + b_ref[...]
        logf = jnp.minimum(z, 0.0) - jnp.log(1.0 + jnp.exp(-jnp.abs(z)))
        hi = logf.astype(BF16)
        r1 = logf - hi.astype(F32)
        mid = r1.astype(BF16)
        lo = (r1 - mid.astype(F32)).astype(BF16)
        local = (jnp.dot(tri, hi, preferred_element_type=F32)
                 + jnp.dot(tri, mid, preferred_element_type=F32)
                 + jnp.dot(tri, lo, preferred_element_type=F32))
        tot = local + carry
        o_ref[rows, :] = tot * LOG2E
        return tot[BLOCK - 1:BLOCK, :]

    lax.fori_loop(0, S // BLOCK, body, jnp.zeros((1, F_PAD), F32))


def _fcumsum(f_slab, b_pad):
    B, S, _ = f_slab.shape
    blk = pl.BlockSpec((None, S, F_PAD), lambda b: (b, 0, 0))
    return pl.pallas_call(
        _fcumsum_kernel,
        grid=(B,),
        in_specs=[blk, pl.BlockSpec((1, F_PAD), lambda b: (0, 0))],
        out_specs=blk,
        out_shape=jax.ShapeDtypeStruct((B, S, F_PAD), F32),
        compiler_params=_cparams(("parallel",)),
        name="fcumsum",
    )(f_slab, b_pad)


def _head_rms_gain(o, is_h0, gain):
    sq = o * o
    ms0 = jnp.sum(jnp.where(is_h0, sq, 0.0), axis=-1, keepdims=True) * (1.0 / HEAD_DIM)
    ms1 = jnp.sum(jnp.where(is_h0, 0.0, sq), axis=-1, keepdims=True) * (1.0 / HEAD_DIM)
    inv = jnp.where(is_h0, lax.rsqrt(ms0 + RMS_EPS), lax.rsqrt(ms1 + RMS_EPS))
    return o * inv * gain


_NT = (((1,), (1,)), ((), ()))


def _fox_kernel(q_ref, k_ref, v_ref, f_ref, gn_ref, o_ref, q_s, v_s, m_s, acc_s, *, tq):
    i = pl.program_id(1)
    S = k_ref.shape[0]
    lane = lax.broadcasted_iota(jnp.int32, (tq, LANES), 1)
    is_h0 = lane < HEAD_DIM
    own = (is_h0, jnp.logical_not(is_h0))
    den_lane = (HEAD_DIM, 0)

    @pl.when(i == 0)
    def _():
        def fill(n, _):
            rows = pl.ds(pl.multiple_of(n * tq, tq), tq)
            for p in range(N_PAIRS):
                vp = v_ref[rows, p * LANES:(p + 1) * LANES].astype(F32)
                for hh in range(2):
                    h = 2 * p + hh
                    ones = jnp.where(lane == den_lane[hh], 1.0, 0.0)
                    v_s[rows, h * LANES:(h + 1) * LANES] = jnp.where(own[hh], vp, ones).astype(BF16)
            return 0
        lax.fori_loop(0, S // tq, fill, 0)

    for p in range(N_PAIRS):
        qp = q_ref[:, p * LANES:(p + 1) * LANES].astype(F32)
        for hh in range(2):
            h = 2 * p + hh
            q_s[:, h * LANES:(h + 1) * LANES] = jnp.where(own[hh], qp, 0.0).astype(BF16)
    m_s[...] = jnp.full(m_s.shape, NEG, F32)
    acc_s[...] = jnp.zeros(acc_s.shape, F32)

    row = lax.broadcasted_iota(jnp.int32, (tq, tq), 0)
    col = lax.broadcasted_iota(jnp.int32, (tq, tq), 1)
    causal = col <= row
    fq = tuple(f_ref[h, i, :, 0:1] for h in range(N_HEADS))

    def tile(kb, masked):
        rows = pl.ds(pl.multiple_of(kb * tq, tq), tq)
        for h in range(N_HEADS):
            hs = slice(h * LANES, (h + 1) * LANES)
            p = h // 2
            k = k_ref[rows, p * LANES:(p + 1) * LANES]
            s = lax.dot_general(q_s[:, hs], k, _NT, preferred_element_type=F32)
            s = s + (fq[h] - f_ref[h, kb])
            if masked:
                s = jnp.where(causal, s, NEG)
            m_old = m_s[h]
            m_new = jnp.maximum(m_old, jnp.max(s, axis=-1, keepdims=True))
            alpha = jnp.exp2(m_old - m_new)
            pm = jnp.exp2(s - jnp.concatenate([m_new] * (tq // LANES), axis=1))
            pv = jnp.dot(pm.astype(BF16), v_s[rows, hs], preferred_element_type=F32)
            acc_s[h] = alpha * acc_s[h] + pv
            m_s[h] = m_new

    def body(j, _):
        tile(2 * j, False)
        tile(2 * j + 1, False)
        return 0

    lax.fori_loop(0, lax.shift_right_logical(i, 1), body, 0)

    @pl.when((i & 1) == 1)
    def _():
        tile(i - 1, False)

    tile(i, True)

    for p in range(N_PAIRS):
        o = []
        for hh in range(2):
            acc = acc_s[2 * p + hh]
            den = jnp.sum(jnp.where(lane == den_lane[hh], acc, 0.0), axis=-1, keepdims=True)
            o.append(acc / den)
        o = jnp.where(is_h0, o[0], o[1])
        cs = slice(p * LANES, (p + 1) * LANES)
        o_ref[:, cs] = _head_rms_gain(o, is_h0, gn_ref[:, cs]).astype(BF16)


def _fox(q, k, v, f_blocks, gn, tq):
    B, S, _ = q.shape
    full = pl.BlockSpec((None, S, WIDTH), lambda b, i: (b, 0, 0))
    return pl.pallas_call(
        functools.partial(_fox_kernel, tq=tq),
        grid=(B, S // tq),
        in_specs=[pl.BlockSpec((None, tq, WIDTH), lambda b, i: (b, i, 0)),
                  full, full,
                  pl.BlockSpec((None, N_HEADS, S // tq, 1, tq), lambda b, i: (b, 0, 0, 0, 0)),
                  pl.BlockSpec((1, WIDTH), lambda b, i: (0, 0))],
        out_specs=pl.BlockSpec((None, tq, WIDTH), lambda b, i: (b, i, 0)),
        out_shape=jax.ShapeDtypeStruct((B, S, WIDTH), BF16),
        scratch_shapes=[pltpu.VMEM((tq, N_HEADS * LANES), BF16),
                        pltpu.VMEM((S, N_HEADS * LANES), BF16),
                        pltpu.VMEM((N_HEADS, tq, LANES), F32),
                        pltpu.VMEM((N_HEADS, tq, LANES), F32)],
        compiler_params=_cparams(("parallel", "arbitrary")),
        name="fox",
    )(q, k, v, f_blocks, gn)


def _dil_kernel(q_ref, k_ref, v_ref, gn_ref, o_ref, qf, kf, vf, acc_s, m_s):
    S = q_ref.shape[0]
    chunk = 2 * BLOCK
    lane_c = lax.broadcasted_iota(jnp.int32, (chunk, LANES), 1)
    is_h0_c = lane_c < HEAD_DIM
    own = (is_h0_c, jnp.logical_not(is_h0_c))
    den_lane = (HEAD_DIM, 0)

    def fill(n, _):
        rows = pl.ds(pl.multiple_of(n * chunk, chunk), chunk)
        q = q_ref[rows, :].astype(F32)
        v = v_ref[rows, :].astype(F32)
        kf[rows, :] = k_ref[rows, :].astype(F32)
        for hh in range(2):
            qf[hh, rows, :] = jnp.where(own[hh], q, 0.0)
            vf[hh, rows, :] = jnp.where(own[hh], v, jnp.where(lane_c == den_lane[hh], 1.0, 0.0))
        return 0

    lax.fori_loop(0, S // chunk, fill, 0)

    def blocks(descs, d, steps, first):
        for start, kstart, nk in descs:
            if d == 1:
                rows_q, rows_k = pl.ds(start, BLOCK), pl.ds(kstart, nk)
            else:
                rows_q, rows_k = pl.ds(start, BLOCK, stride=d), pl.ds(kstart, nk, stride=d)
            qi = lax.broadcasted_iota(jnp.int32, (BLOCK, nk), 0)
            kj = lax.broadcasted_iota(jnp.int32, (BLOCK, nk), 1)
            dist = qi + (nk - BLOCK) - kj
            mask = (dist >= 0) & (dist <= steps)
            k = kf[rows_k, :].astype(BF16)
            for hh in range(2):
                q = qf[hh, rows_q, :].astype(BF16)
                v = vf[hh, rows_k, :].astype(BF16)
                s = lax.dot_general(q, k, _NT, preferred_element_type=F32)
                s = jnp.where(mask, s, NEG)
                m_new = jnp.broadcast_to(jnp.max(s, axis=-1, keepdims=True), (BLOCK, LANES))
                if not first:
                    m_old = m_s[hh, rows_q, :]
                    m_new = jnp.maximum(m_old, m_new)
                    alpha = jnp.exp2(m_old - m_new)
                pm = jnp.exp2(s - jnp.concatenate([m_new] * (nk // LANES), axis=1))
                acc = jnp.dot(pm.astype(BF16), v, preferred_element_type=F32)
                if not first:
                    acc = acc + alpha * acc_s[hh, rows_q, :]
                acc_s[hh, rows_q, :] = acc
                m_s[hh, rows_q, :] = m_new

    group = 4
    for pi, (window, d) in enumerate(DILATED_PATTERNS):
        steps = window // d
        nb = S // d // BLOCK
        first = pi == 0
        span = d * BLOCK
        head_block = lambda r: (r, r, BLOCK)
        next_block = lambda r, base, span=span: (r + base + span, r + base, 2 * BLOCK)
        if d == 1:
            blocks([head_block(0)] + [next_block(0, j * span) for j in range(group - 1)],
                   d, steps, first)

            def body(g, _, d=d, steps=steps, first=first, span=span, next_block=next_block):
                base = pl.multiple_of((g * group - 1) * span, BLOCK)
                blocks([next_block(0, base + j * span) for j in range(group)], d, steps, first)
                return 0

            lax.fori_loop(1, nb // group, body, 0)
        elif nb > 2:
            for r0 in range(0, d, group):
                blocks([head_block(r0 + j) for j in range(group)], d, steps, first)

                def body(blk, _, r0=r0, d=d, steps=steps, first=first, span=span,
                         next_block=next_block):
                    base = (blk - 1) * span
                    blocks([next_block(r0 + j, base) for j in range(group)], d, steps, first)
                    return 0

                lax.fori_loop(1, nb, body, 0)
        else:
            def body(g, _, d=d, steps=steps, first=first, next_block=next_block):
                r0 = g * group
                blocks([head_block(r0 + j) for j in range(group)]
                       + [next_block(r0 + j, 0) for j in range(group)], d, steps, first)
                return 0

            lax.fori_loop(0, d // group, body, 0)

    def finish(n, _):
        rows = pl.ds(pl.multiple_of(n * chunk, chunk), chunk)
        o = []
        for hh in range(2):
            acc = acc_s[hh, rows, :]
            den = jnp.sum(jnp.where(lane_c == den_lane[hh], acc, 0.0), axis=-1, keepdims=True)
            o.append(acc / den)
        o = jnp.where(is_h0_c, o[0], o[1])
        o_ref[rows, :] = _head_rms_gain(o, is_h0_c, gn_ref[...]).astype(BF16)
        return 0

    lax.fori_loop(0, S // chunk, finish, 0)


def _dilated(q, k, v, gn):
    B, S, _ = q.shape
    blk = pl.BlockSpec((None, S, LANES), lambda b, p: (b, 0, p))
    pair = pltpu.VMEM((2, S, LANES), F32)
    return pl.pallas_call(
        _dil_kernel,
        grid=(B, N_PAIRS),
        in_specs=[blk, blk, blk, pl.BlockSpec((1, LANES), lambda b, p: (0, p))],
        out_specs=blk,
        out_shape=jax.ShapeDtypeStruct((B, S, WIDTH), BF16),
        scratch_shapes=[pair, pltpu.VMEM((S, LANES), F32), pair, pair, pair],
        compiler_params=_cparams(("parallel", "parallel")),
        name="dilated",
    )(q, k, v, gn)


def _layer_norm(x, g, b):
    mu = jnp.mean(x, axis=-1, keepdims=True)
    xc = x - mu
    var = jnp.mean(xc * xc, axis=-1, keepdims=True)
    return xc * lax.rsqrt(var + LN_EPS) * g + b


def _outproj_kernel(oa_ref, ob_ref, x_ref, wo_ref, ga_ref, lg_ref, lb_ref, sc_ref, sh_ref,
                    x1_ref, h2_ref, *, alpha):
    mix = (jnp.dot(oa_ref[...], wo_ref[0:WIDTH, :], preferred_element_type=F32)
           + jnp.dot(ob_ref[...], wo_ref[WIDTH:2 * WIDTH, :], preferred_element_type=F32))
    x1 = _layer_norm(alpha * x_ref[...] + ga_ref[...] * mix, lg_ref[...], lb_ref[...])
    x1_ref[...] = x1
    h2_ref[...] = (x1 * (1.0 + sc_ref[...]) + sh_ref[...]).astype(BF16)


def _outproj(oa, ob, x, w_out, g_a, ln_g, ln_b, sc_f, sh_f, alpha, ts):
    B, S, D = x.shape
    row = lambda b, i: (b, i, 0)
    per_b = lambda b, i: (b, 0, 0)
    const = lambda b, i: (0, 0)
    vec_b = pl.BlockSpec((None, 1, D), per_b)
    vec = pl.BlockSpec((1, D), const)
    return pl.pallas_call(
        functools.partial(_outproj_kernel, alpha=alpha),
        grid=(B, S // ts),
        in_specs=[pl.BlockSpec((None, ts, WIDTH), row),
                  pl.BlockSpec((None, ts, WIDTH), row),
                  pl.BlockSpec((None, ts, D), row),
                  pl.BlockSpec((2 * WIDTH, D), const),
                  vec_b, vec, vec, vec_b, vec_b],
        out_specs=[pl.BlockSpec((None, ts, D), row)] * 2,
        out_shape=[jax.ShapeDtypeStruct((B, S, D), F32), jax.ShapeDtypeStruct((B, S, D), BF16)],
        compiler_params=_cparams(("parallel", "parallel")),
        name="outproj",
    )(oa, ob, x, w_out, g_a, ln_g, ln_b, sc_f, sh_f)


def _ffn_kernel(h_ref, halo_ref, x1_ref, wup_ref, cw_ref, cb_ref, wdn_ref, gf_ref, lg_ref, lb_ref,
                o_ref, hcat, u_s, act, *, alpha, n_chunks):
    ts = h_ref.shape[0]
    i = pl.program_id(1)
    halo = halo_ref[...]
    hcat[0:HALO, :] = jnp.where(i > 0, halo, jnp.zeros_like(halo))
    hcat[HALO:, :] = h_ref[...]
    hc = hcat[...]
    cc = 2 * FFN_CHUNK
    for c in range(n_chunks):
        cols = slice(c * cc, (c + 1) * cc)
        u_s[...] = jnp.dot(hc, wup_ref[:, cols], preferred_element_type=F32)
        cw = cw_ref[:, cols]
        y = cb_ref[:, cols]
        for t in range(CONV_WIDTH):
            off = HALO - (CONV_WIDTH - 1) + t
            y = y + u_s[off:off + ts, :] * cw[t:t + 1, :]
        a = y[:, :FFN_CHUNK]
        g = y[:, FFN_CHUNK:]
        act[:, c * FFN_CHUNK:(c + 1) * FFN_CHUNK] = (g / (1.0 + jnp.exp(-g)) * a).astype(BF16)
    ffn = jnp.dot(act[...], wdn_ref[...], preferred_element_type=F32)
    o_ref[...] = _layer_norm(alpha * x1_ref[...] + gf_ref[...] * ffn, lg_ref[...], lb_ref[...])


def _ffn(h2, x1, w_up_c, cw_c, cb_c, w_down, g_f, ln_g, ln_b, alpha, ts):
    B, S, D = x1.shape
    d_ff = w_down.shape[0]
    n_chunks = d_ff // FFN_CHUNK
    row = lambda b, i: (b, i, 0)
    const = lambda b, i: (0, 0)
    vec = pl.BlockSpec((1, D), const)
    halo_idx = lambda b, i: (b, jnp.maximum(i * (ts // HALO) - 1, 0), 0)
    return pl.pallas_call(
        functools.partial(_ffn_kernel, alpha=alpha, n_chunks=n_chunks),
        grid=(B, S // ts),
        in_specs=[pl.BlockSpec((None, ts, D), row),
                  pl.BlockSpec((None, HALO, D), halo_idx),
                  pl.BlockSpec((None, ts, D), row),
                  pl.BlockSpec((D, 2 * d_ff), const),
                  pl.BlockSpec((CONV_WIDTH, 2 * d_ff), const),
                  pl.BlockSpec((1, 2 * d_ff), const),
                  pl.BlockSpec((d_ff, D), const),
                  pl.BlockSpec((None, 1, D), lambda b, i: (b, 0, 0)),
                  vec, vec],
        out_specs=pl.BlockSpec((None, ts, D), row),
        out_shape=jax.ShapeDtypeStruct((B, S, D), F32),
        scratch_shapes=[pltpu.VMEM((HALO + ts, D), BF16),
                        pltpu.VMEM((HALO + ts, 2 * FFN_CHUNK), F32),
                        pltpu.VMEM((ts, d_ff), BF16)],
        compiler_params=_cparams(("parallel", "parallel")),
        name="ffn",
    )(h2, h2, x1, w_up_c, cw_c, cb_c, w_down, g_f, ln_g, ln_b)


def _interleave_chunks(t, d_ff):
    lead = t.shape[:-1]
    t = t.reshape(lead + (2, d_ff // FFN_CHUNK, FFN_CHUNK))
    t = jnp.swapaxes(t, -3, -2)
    return t.reshape(lead + (2 * d_ff,))


def kernel(x, c, positions, w_ada, b_ada, w_in, b_fgate, gn_a, gn_b, w_out,
           ln1_g, ln1_b, w_up, conv_w, conv_b, w_down, ln2_g, ln2_b):
    B, S, D = x.shape
    depth = w_ada.shape[0]
    d_ff = w_down.shape[1]
    alpha = (2.0 * depth) ** 0.25
    scale = HEAD_DIM ** -0.5
    ts = 512
    tq = 256

    for l in range(depth):
        ada = _ada(c, w_ada[l], b_ada[l])
        sh_a, sc_a, g_a, sh_f, sc_f, g_f = (t[:, None, :] for t in jnp.split(ada, 6, axis=-1))

        w = w_in[l]
        o3 = 3 * WIDTH
        w_f = jnp.pad(w[:, o3:o3 + N_HEADS], ((0, 0), (0, F_PAD - N_HEADS)))
        ob = o3 + N_HEADS
        w_cat = jnp.concatenate(
            [w[:, 0:WIDTH] * scale, w[:, WIDTH:o3],
             w[:, ob:ob + WIDTH] * scale, w[:, ob + WIDTH:ob + 3 * WIDTH], w_f], axis=1).astype(BF16)

        qa, ka, va, qb, kb, vb, f_slab = _inproj(x, sc_a, sh_a, w_cat, positions, ts)

        b_pad = jnp.pad(b_fgate[l], (0, F_PAD - N_HEADS)).reshape(1, F_PAD)
        f_cum = _fcumsum(f_slab, b_pad)
        f_blocks = f_cum[:, :, :N_HEADS].transpose(0, 2, 1).reshape(B, N_HEADS, S // tq, 1, tq)

        oa = _fox(qa, ka, va, f_blocks, gn_a[l].reshape(1, WIDTH), tq)
        ob_ = _dilated(qb, kb, vb, gn_b[l].reshape(1, WIDTH))

        x1, h2 = _outproj(oa, ob_, x, w_out[l].astype(BF16), g_a,
                          ln1_g[l].reshape(1, D), ln1_b[l].reshape(1, D), sc_f, sh_f, alpha, ts)

        x = _ffn(h2, x1,
                 _interleave_chunks(w_up[l], d_ff).astype(BF16),
                 _interleave_chunks(conv_w[l], d_ff),
                 _interleave_chunks(conv_b[l], d_ff).reshape(1, 2 * d_ff),
                 w_down[l].astype(BF16), g_f,
                 ln2_g[l].reshape(1, D), ln2_b[l].reshape(1, D), alpha, ts)
    return x
```

```python
import functools

import jax
import jax.numpy as jnp
from jax import lax
from jax.experimental import pallas as pl
from jax.experimental.pallas import tpu as pltpu

F32 = jnp.float32
BF16 = jnp.bfloat16

HEAD_DIM = 64
N_HEADS = 8
WIDTH = N_HEADS * HEAD_DIM
LANES = 128
N_PAIRS = WIDTH // LANES
BLOCK = 128
DILATED_PATTERNS = ((128, 1), (512, 4), (2048, 16))
FOLD = 4
ROPE_THETA = 500000.0
ROPE_DIMS = HEAD_DIM // 4
ROPE_HALF = ROPE_DIMS // 2
CONV_WIDTH = 3
LN_EPS = 1e-5
RMS_EPS = 1e-6
NEG = -1e30
LOG2E = 1.4426950408889634
F_PAD = LANES
FFN_CHUNK = 256
HALO = 16
VMEM_LIMIT = 56 * 1024 * 1024


def _cparams(sem):
    return pltpu.CompilerParams(dimension_semantics=sem, vmem_limit_bytes=VMEM_LIMIT)


def _ada_kernel(c_ref, w_ref, b_ref, o_ref):
    c = c_ref[...]
    s = c / (1.0 + jnp.exp(-c))
    o_ref[...] = jnp.dot(s, w_ref[...], preferred_element_type=F32,
                         precision=lax.Precision.HIGHEST) + b_ref[...]


def _ada(c, w, b):
    B, D = c.shape
    N = w.shape[1]
    return pl.pallas_call(
        _ada_kernel,
        grid=(N // D,),
        in_specs=[pl.BlockSpec((B, D), lambda j: (0, 0)),
                  pl.BlockSpec((D, D), lambda j: (0, j)),
                  pl.BlockSpec((1, D), lambda j: (0, j))],
        out_specs=pl.BlockSpec((B, D), lambda j: (0, j)),
        out_shape=jax.ShapeDtypeStruct((B, N), F32),
        compiler_params=_cparams(("arbitrary",)),
        name="ada",
    )(c, w, b.reshape(1, N))


def _inproj_kernel(x_ref, sc_ref, sh_ref, w_ref, pos_ref, freq_ref,
                   qa_ref, ka_ref, va_ref, qb_ref, kb_ref, vb_ref, f_ref):
    ts = x_ref.shape[0]
    h = (x_ref[...] * (1.0 + sc_ref[...]) + sh_ref[...]).astype(BF16)

    def proj(g, width=WIDTH):
        return jnp.dot(h, w_ref[:, g * WIDTH:g * WIDTH + width], preferred_element_type=F32)

    qa_ref[...] = (proj(0) * LOG2E).astype(BF16)
    ka_ref[...] = proj(1).astype(BF16)
    va_ref[...] = proj(2).astype(BF16)
    vb_ref[...] = proj(5).astype(BF16)
    f_ref[...] = proj(6, F_PAD)

    n_sub = ts // LANES
    p_rows = pos_ref[...]
    p_cols = jnp.concatenate([p_rows, jnp.zeros((8 - n_sub, LANES), F32)], axis=0).T
    ang = jnp.concatenate([p_cols[:, j:j + 1] * freq_ref[...] for j in range(n_sub)], axis=0)
    cosv = jnp.cos(ang)
    sinv = jnp.sin(ang)
    lane = lax.broadcasted_iota(jnp.int32, (ts, LANES), 1)
    first_half = (lane & (HEAD_DIM - 1)) < ROPE_HALF

    def rotary(t):
        nxt = pltpu.roll(t, LANES - ROPE_HALF, 1)
        prv = pltpu.roll(t, ROPE_HALF, 1)
        return t * cosv + jnp.where(first_half, -nxt, prv) * sinv

    for g, ref, gain in ((3, qb_ref, LOG2E), (4, kb_ref, None)):
        z = proj(g)
        for p in range(N_PAIRS):
            cs = slice(p * LANES, (p + 1) * LANES)
            r = rotary(z[:, cs])
            ref[:, cs] = (r if gain is None else r * gain).astype(BF16)


def _inproj(x, sc, sh, w_cat, positions, ts):
    B, S, D = x.shape
    n_cols = w_cat.shape[1]
    row = lambda b, i: (b, i, 0)
    per_b = lambda b, i: (b, 0, 0)
    wide = pl.BlockSpec((None, ts, WIDTH), row)
    tab = pl.BlockSpec((None, ts, LANES), row)
    pos = positions.astype(F32).reshape(B, S // ts, ts // LANES, LANES)
    freqs = ROPE_THETA ** (-jnp.arange(0, ROPE_DIMS, 2, dtype=F32) / ROPE_DIMS)
    head = jnp.concatenate([freqs, freqs, jnp.zeros((HEAD_DIM - ROPE_DIMS,), F32)])
    lane_freq = jnp.concatenate([head, head]).reshape(1, LANES)
    return pl.pallas_call(
        _inproj_kernel,
        grid=(B, S // ts),
        in_specs=[pl.BlockSpec((None, ts, D), row),
                  pl.BlockSpec((None, 1, D), per_b),
                  pl.BlockSpec((None, 1, D), per_b),
                  pl.BlockSpec((D, n_cols), lambda b, i: (0, 0)),
                  pl.BlockSpec((None, None, ts // LANES, LANES), lambda b, i: (b, i, 0, 0)),
                  pl.BlockSpec((1, LANES), lambda b, i: (0, 0))],
        out_specs=[wide] * 6 + [tab],
        out_shape=[jax.ShapeDtypeStruct((B, S, WIDTH), BF16)] * 6
                  + [jax.ShapeDtypeStruct((B, S, F_PAD), F32)],
        compiler_params=_cparams(("parallel", "parallel")),
        name="inproj",
    )(x, sc, sh, w_cat, pos, lane_freq)


def _fcumsum_kernel(f_ref, b_ref, o_ref):
    S = f_ref.shape[0]
    r = lax.broadcasted_iota(jnp.int32, (BLOCK, BLOCK), 0)
    c = lax.broadcasted_iota(jnp.int32, (BLOCK, BLOCK), 1)
    tri = jnp.where(c <= r, 1.0, 0.0).astype(BF16)

    def body(n, carry):
        rows = pl.ds(pl.multiple_of(n * BLOCK, BLOCK), BLOCK)
        z = f_ref[rows, :] + b_ref[...]
        logf = jnp.minimum(z, 0.0) - jnp.log(1.0 + jnp.exp(-jnp.abs(z)))
        hi = logf.astype(BF16)
        r1 = logf - hi.astype(F32)
        mid = r1.astype(BF16)
        lo = (r1 - mid.astype(F32)).astype(BF16)
        local = (jnp.dot(tri, hi, preferred_element_type=F32)
                 + jnp.dot(tri, mid, preferred_element_type=F32)
                 + jnp.dot(tri, lo, preferred_element_type=F32))
        tot = local + carry
        o_ref[rows, :] = tot * LOG2E
        return tot[BLOCK - 1:BLOCK, :]

    lax.fori_loop(0, S // BLOCK, body, jnp.zeros((1, F_PAD), F32))


def _fcumsum(f_slab, b_pad):
    B, S, _ = f_slab.shape
    blk = pl.BlockSpec((None, S, F_PAD), lambda b: (b, 0, 0))
    return pl.pallas_call(
        _fcumsum_kernel,
        grid=(B,),
        in_specs=[blk, pl.BlockSpec((1, F_PAD), lambda b: (0, 0))],
        out_specs=blk,
        out_shape=jax.ShapeDtypeStruct((B, S, F_PAD), F32),
        compiler_params=_cparams(("parallel",)),
        name="fcumsum",
    )(f_slab, b_pad)


def _head_rms_gain(o, is_h0, gain):
    sq = o * o
    ms0 = jnp.sum(jnp.where(is_h0, sq, 0.0), axis=-1, keepdims=True) * (1.0 / HEAD_DIM)
    ms1 = jnp.sum(jnp.where(is_h0, 0.0, sq), axis=-1, keepdims=True) * (1.0 / HEAD_DIM)
    inv = jnp.where(is_h0, lax.rsqrt(ms0 + RMS_EPS), lax.rsqrt(ms1 + RMS_EPS))
    return o * inv * gain


_NT = (((1,), (1,)), ((), ()))


def _fox_kernel(q_ref, k_ref, v_ref, f_ref, gn_ref, o_ref, q_s, v_s, m_s, acc_s, *, tq):
    i = pl.program_id(1)
    S = k_ref.shape[0]
    lane = lax.broadcasted_iota(jnp.int32, (tq, LANES), 1)
    is_h0 = lane < HEAD_DIM
    own = (is_h0, jnp.logical_not(is_h0))
    den_lane = (HEAD_DIM, 0)

    @pl.when(i == 0)
    def _():
        def fill(n, _):
            rows = pl.ds(pl.multiple_of(n * tq, tq), tq)
            for p in range(N_PAIRS):
                vp = v_ref[rows, p * LANES:(p + 1) * LANES].astype(F32)
                for hh in range(2):
                    h = 2 * p + hh
                    ones = jnp.where(lane == den_lane[hh], 1.0, 0.0)
                    v_s[rows, h * LANES:(h + 1) * LANES] = jnp.where(own[hh], vp, ones).astype(BF16)
            return 0
        lax.fori_loop(0, S // tq, fill, 0)

    for p in range(N_PAIRS):
        qp = q_ref[:, p * LANES:(p + 1) * LANES].astype(F32)
        for hh in range(2):
            h = 2 * p + hh
            q_s[:, h * LANES:(h + 1) * LANES] = jnp.where(own[hh], qp, 0.0).astype(BF16)
    m_s[...] = jnp.full(m_s.shape, NEG, F32)
    acc_s[...] = jnp.zeros(acc_s.shape, F32)

    row = lax.broadcasted_iota(jnp.int32, (tq, tq), 0)
    col = lax.broadcasted_iota(jnp.int32, (tq, tq), 1)
    causal = col <= row
    fq = tuple(f_ref[h, i, :, 0:1] for h in range(N_HEADS))

    def tile(kb, masked):
        rows = pl.ds(pl.multiple_of(kb * tq, tq), tq)
        for h in range(N_HEADS):
            hs = slice(h * LANES, (h + 1) * LANES)
            p = h // 2
            k = k_ref[rows, p * LANES:(p + 1) * LANES]
            s = lax.dot_general(q_s[:, hs], k, _NT, preferred_element_type=F32)
            s = s + (fq[h] - f_ref[h, kb])
            if masked:
                s = jnp.where(causal, s, NEG)
            m_old = m_s[h]
            m_new = jnp.maximum(m_old, jnp.max(s, axis=-1, keepdims=True))
            alpha = jnp.exp2(m_old - m_new)
            pm = jnp.exp2(s - jnp.concatenate([m_new] * (tq // LANES), axis=1))
            pv = jnp.dot(pm.astype(BF16), v_s[rows, hs], preferred_element_type=F32)
            acc_s[h] = alpha * acc_s[h] + pv
            m_s[h] = m_new

    def body(j, _):
        for u in range(4):
            tile(4 * j + u, False)
        return 0

    lax.fori_loop(0, lax.shift_right_logical(i, 2), body, 0)
    done = i - (i & 3)

    @pl.when((i & 2) != 0)
    def _():
        tile(done, False)
        tile(done + 1, False)

    @pl.when((i & 1) != 0)
    def _():
        tile(i - 1, False)

    tile(i, True)

    for p in range(N_PAIRS):
        o = []
        for hh in range(2):
            acc = acc_s[2 * p + hh]
            den = jnp.sum(jnp.where(lane == den_lane[hh], acc, 0.0), axis=-1, keepdims=True)
            o.append(acc / den)
        o = jnp.where(is_h0, o[0], o[1])
        cs = slice(p * LANES, (p + 1) * LANES)
        o_ref[:, cs] = _head_rms_gain(o, is_h0, gn_ref[:, cs]).astype(BF16)


def _fox(q, k, v, f_blocks, gn, tq):
    B, S, _ = q.shape
    full = pl.BlockSpec((None, S, WIDTH), lambda b, i: (b, 0, 0))
    return pl.pallas_call(
        functools.partial(_fox_kernel, tq=tq),
        grid=(B, S // tq),
        in_specs=[pl.BlockSpec((None, tq, WIDTH), lambda b, i: (b, i, 0)),
                  full, full,
                  pl.BlockSpec((None, N_HEADS, S // tq, 1, tq), lambda b, i: (b, 0, 0, 0, 0)),
                  pl.BlockSpec((1, WIDTH), lambda b, i: (0, 0))],
        out_specs=pl.BlockSpec((None, tq, WIDTH), lambda b, i: (b, i, 0)),
        out_shape=jax.ShapeDtypeStruct((B, S, WIDTH), BF16),
        scratch_shapes=[pltpu.VMEM((tq, N_HEADS * LANES), BF16),
                        pltpu.VMEM((S, N_HEADS * LANES), BF16),
                        pltpu.VMEM((N_HEADS, tq, LANES), F32),
                        pltpu.VMEM((N_HEADS, tq, LANES), F32)],
        compiler_params=_cparams(("parallel", "arbitrary")),
        name="fox",
    )(q, k, v, f_blocks, gn)


def _dil_kernel(q_ref, k_ref, v_ref, gn_ref, o_ref, nat, qL, kL, vL, accL, mL):
    S = q_ref.shape[0]
    R = S // FOLD
    chunk = 2 * BLOCK
    lane_c = lax.broadcasted_iota(jnp.int32, (chunk, LANES), 1)
    is_h0_c = lane_c < HEAD_DIM
    own = (is_h0_c, jnp.logical_not(is_h0_c))
    den_lane = (HEAD_DIM, 0)

    def fold(src_ref, store):
        def widen(n, _):
            rows = pl.ds(pl.multiple_of(n * chunk, chunk), chunk)
            nat[rows, :] = src_ref[rows, :].astype(F32)
            return 0

        lax.fori_loop(0, S // chunk, widen, 0)

        def spread(n, _):
            rows = pl.ds(pl.multiple_of(n * chunk, chunk), chunk)
            for j in range(FOLD):
                store(j, rows, nat[pl.ds(n * chunk * FOLD + j, chunk, stride=FOLD), :])
            return 0

        lax.fori_loop(0, R // chunk, spread, 0)

    def store_q(j, rows, x):
        for hh in range(2):
            qL[hh, j, rows, :] = jnp.where(own[hh], x, 0.0)

    def store_k(j, rows, x):
        kL[j, rows, :] = x

    def store_v(j, rows, x):
        for hh in range(2):
            vL[hh, j, rows, :] = jnp.where(own[hh], x, jnp.where(lane_c == den_lane[hh], 1.0, 0.0))

    fold(q_ref, store_q)
    fold(k_ref, store_k)
    fold(v_ref, store_v)

    def band_mask(nq, nk, off, steps, interleaved):
        qi = lax.broadcasted_iota(jnp.int32, (nq, nk), 0)
        kj = lax.broadcasted_iota(jnp.int32, (nq, nk), 1)
        if interleaved:
            sq, sk = nq // FOLD, nk // FOLD
            qi = FOLD * (qi % sq) + qi // sq
            kj = FOLD * (kj % sk) + kj // sk
        dist = qi + off - kj
        return (dist >= 0) & (dist <= steps)

    def tile_of(ref, lead, segs):
        parts = [ref[lead + (j, rows, slice(None))] for j, rows in segs]
        return parts[0] if len(parts) == 1 else jnp.concatenate(parts, axis=0)

    def blocks(descs, steps, first, interleaved):
        for q_segs, k_segs, off in descs:
            k = tile_of(kL, (), k_segs).astype(BF16)
            nk = k.shape[0]
            for hh in range(2):
                q = tile_of(qL, (hh,), q_segs).astype(BF16)
                nq = q.shape[0]
                seg = nq // len(q_segs)
                v = tile_of(vL, (hh,), k_segs).astype(BF16)
                s = lax.dot_general(q, k, _NT, preferred_element_type=F32)
                s = jnp.where(band_mask(nq, nk, off, steps, interleaved), s, NEG)
                m_new = jnp.broadcast_to(jnp.max(s, axis=-1, keepdims=True), (nq, LANES))
                if not first:
                    m_old = tile_of(mL, (hh,), q_segs)
                    m_new = jnp.maximum(m_old, m_new)
                    alpha = jnp.exp2(m_old - m_new)
                pm = jnp.exp2(s - jnp.concatenate([m_new] * (nk // LANES), axis=1))
                acc = jnp.dot(pm.astype(BF16), v, preferred_element_type=F32)
                if not first:
                    acc = acc + alpha * tile_of(accL, (hh,), q_segs)
                for n, (j, rows) in enumerate(q_segs):
                    accL[hh, j, rows, :] = acc[n * seg:(n + 1) * seg, :]
                    mL[hh, j, rows, :] = m_new[n * seg:(n + 1) * seg, :]

    wide = 2 * BLOCK
    group = 8
    for pi, (window, d) in enumerate(DILATED_PATTERNS[::-1]):
        steps = window // d
        nb = S // d // BLOCK
        first = pi == 0
        if d == 1:
            seg = BLOCK // FOLD

            def span(row0, rows):
                return [(j, pl.ds(row0, rows)) for j in range(FOLD)]

            opening = (span(0, 2 * seg), span(0, 2 * seg), 0)
            blocks([opening] + [(span(seg * n, seg), span(seg * (n - 1), 2 * seg), BLOCK)
                                for n in range(2, group)], steps, first, True)

            def body(g, _, steps=steps, first=first, span=span, seg=seg):
                base = pl.multiple_of(g * group * seg, group * seg)
                blocks([(span(base + seg * n, seg), span(base + seg * (n - 1), 2 * seg), BLOCK)
                        for n in range(group)], steps, first, True)
                return 0

            lax.fori_loop(1, nb // group, body, 0)
        elif d == FOLD:
            per = group // d
            blocks([([(j, pl.ds(0, wide))], [(j, pl.ds(0, wide))], 0) for j in range(d)],
                   steps, first, False)

            def body(g, _, steps=steps, first=first, per=per, d=d):
                descs = []
                for j in range(d):
                    for n in range(per):
                        start = pl.multiple_of((g * per + n) * BLOCK, BLOCK)
                        descs.append(([(j, pl.ds(start, BLOCK))],
                                      [(j, pl.ds(start - BLOCK, wide))], BLOCK))
                blocks(descs, steps, first, False)
                return 0

            lax.fori_loop(1, nb // per, body, 0)
        else:
            sub = d // FOLD
            assert nb * BLOCK == wide

            def body(jj, _, steps=steps, first=first, sub=sub):
                tile = lambda j: [(j, pl.ds(jj, wide, stride=sub))]
                blocks([(tile(j), tile(j), 0) for j in range(FOLD)], steps, first, False)
                return 0

            lax.fori_loop(0, sub, body, 0)

    def finish(n, _):
        rows = pl.ds(pl.multiple_of(n * chunk, chunk), chunk)
        for j in range(FOLD):
            o = []
            for hh in range(2):
                acc = accL[hh, j, rows, :]
                den = jnp.sum(jnp.where(lane_c == den_lane[hh], acc, 0.0), axis=-1, keepdims=True)
                o.append(acc / den)
            o = jnp.where(is_h0_c, o[0], o[1])
            nat[pl.ds(n * chunk * FOLD + j, chunk, stride=FOLD), :] = _head_rms_gain(
                o, is_h0_c, gn_ref[...])
        return 0

    lax.fori_loop(0, R // chunk, finish, 0)

    def narrow(n, _):
        rows = pl.ds(pl.multiple_of(n * chunk, chunk), chunk)
        o_ref[rows, :] = nat[rows, :].astype(BF16)
        return 0

    lax.fori_loop(0, S // chunk, narrow, 0)


def _dilated(q, k, v, gn):
    B, S, _ = q.shape
    blk = pl.BlockSpec((None, S, LANES), lambda b, p: (b, 0, p))
    folded = (FOLD, S // FOLD, LANES)
    pair = pltpu.VMEM((2,) + folded, F32)
    return pl.pallas_call(
        _dil_kernel,
        grid=(B, N_PAIRS),
        in_specs=[blk, blk, blk, pl.BlockSpec((1, LANES), lambda b, p: (0, p))],
        out_specs=blk,
        out_shape=jax.ShapeDtypeStruct((B, S, WIDTH), BF16),
        scratch_shapes=[pltpu.VMEM((S, LANES), F32), pair, pltpu.VMEM(folded, F32),
                        pair, pair, pair],
        compiler_params=_cparams(("parallel", "parallel")),
        name="dilated",
    )(q, k, v, gn)


def _layer_norm(x, g, b):
    mu = jnp.mean(x, axis=-1, keepdims=True)
    xc = x - mu
    var = jnp.mean(xc * xc, axis=-1, keepdims=True)
    return xc * lax.rsqrt(var + LN_EPS) * g + b


def _outproj_kernel(oa_ref, ob_ref, x_ref, wo_ref, ga_ref, lg_ref, lb_ref, sc_ref, sh_ref,
                    x1_ref, h2_ref, *, alpha):
    mix = (jnp.dot(oa_ref[...], wo_ref[0:WIDTH, :], preferred_element_type=F32)
           + jnp.dot(ob_ref[...], wo_ref[WIDTH:2 * WIDTH, :], preferred_element_type=F32))
    x1 = _layer_norm(alpha * x_ref[...] + ga_ref[...] * mix, lg_ref[...], lb_ref[...])
    x1_ref[...] = x1
    h2_ref[...] = (x1 * (1.0 + sc_ref[...]) + sh_ref[...]).astype(BF16)


def _outproj(oa, ob, x, w_out, g_a, ln_g, ln_b, sc_f, sh_f, alpha, ts):
    B, S, D = x.shape
    row = lambda b, i: (b, i, 0)
    per_b = lambda b, i: (b, 0, 0)
    const = lambda b, i: (0, 0)
    vec_b = pl.BlockSpec((None, 1, D), per_b)
    vec = pl.BlockSpec((1, D), const)
    return pl.pallas_call(
        functools.partial(_outproj_kernel, alpha=alpha),
        grid=(B, S // ts),
        in_specs=[pl.BlockSpec((None, ts, WIDTH), row),
                  pl.BlockSpec((None, ts, WIDTH), row),
                  pl.BlockSpec((None, ts, D), row),
                  pl.BlockSpec((2 * WIDTH, D), const),
                  vec_b, vec, vec, vec_b, vec_b],
        out_specs=[pl.BlockSpec((None, ts, D), row)] * 2,
        out_shape=[jax.ShapeDtypeStruct((B, S, D), F32), jax.ShapeDtypeStruct((B, S, D), BF16)],
        compiler_params=_cparams(("parallel", "parallel")),
        name="outproj",
    )(oa, ob, x, w_out, g_a, ln_g, ln_b, sc_f, sh_f)


def _ffn_kernel(h_ref, halo_ref, x1_ref, wup_ref, cw_ref, cb_ref, wdn_ref, gf_ref, lg_ref, lb_ref,
                o_ref, hcat, u_s, act, *, alpha, n_chunks):
    ts = h_ref.shape[0]
    i = pl.program_id(1)
    halo = halo_ref[...]
    hcat[0:HALO, :] = jnp.where(i > 0, halo, jnp.zeros_like(halo))
    hcat[HALO:, :] = h_ref[...]
    cc = 2 * FFN_CHUNK
    for c in range(n_chunks):
        cols = slice(c * cc, (c + 1) * cc)
        u = u_s.at[c % 2]
        u[...] = jnp.dot(hcat[...], wup_ref[:, cols], preferred_element_type=F32)
        cw = cw_ref[:, cols]
        y = cb_ref[:, cols]
        for t in range(CONV_WIDTH):
            off = HALO - (CONV_WIDTH - 1) + t
            y = y + u[off:off + ts, :] * cw[t:t + 1, :]
        a = y[:, :FFN_CHUNK]
        g = y[:, FFN_CHUNK:]
        act[:, c * FFN_CHUNK:(c + 1) * FFN_CHUNK] = (g / (1.0 + jnp.exp(-g)) * a).astype(BF16)
    ffn = jnp.dot(act[...], wdn_ref[...], preferred_element_type=F32)
    o_ref[...] = _layer_norm(alpha * x1_ref[...] + gf_ref[...] * ffn, lg_ref[...], lb_ref[...])


def _ffn(h2, x1, w_up_c, cw_c, cb_c, w_down, g_f, ln_g, ln_b, alpha, ts):
    B, S, D = x1.shape
    d_ff = w_down.shape[0]
    n_chunks = d_ff // FFN_CHUNK
    row = lambda b, i: (b, i, 0)
    const = lambda b, i: (0, 0)
    vec = pl.BlockSpec((1, D), const)
    halo_idx = lambda b, i: (b, jnp.maximum(i * (ts // HALO) - 1, 0), 0)
    return pl.pallas_call(
        functools.partial(_ffn_kernel, alpha=alpha, n_chunks=n_chunks),
        grid=(B, S // ts),
        in_specs=[pl.BlockSpec((None, ts, D), row),
                  pl.BlockSpec((None, HALO, D), halo_idx),
                  pl.BlockSpec((None, ts, D), row),
                  pl.BlockSpec((D, 2 * d_ff), const, pipeline_mode=pl.Buffered(1)),
                  pl.BlockSpec((CONV_WIDTH, 2 * d_ff), const),
                  pl.BlockSpec((1, 2 * d_ff), const),
                  pl.BlockSpec((d_ff, D), const, pipeline_mode=pl.Buffered(1)),
                  pl.BlockSpec((None, 1, D), lambda b, i: (b, 0, 0)),
                  vec, vec],
        out_specs=pl.BlockSpec((None, ts, D), row),
        out_shape=jax.ShapeDtypeStruct((B, S, D), F32),
        scratch_shapes=[pltpu.VMEM((HALO + ts, D), BF16),
                        pltpu.VMEM((2, HALO + ts, 2 * FFN_CHUNK), F32),
                        pltpu.VMEM((ts, d_ff), BF16)],
        compiler_params=_cparams(("parallel", "parallel")),
        name="ffn",
    )(h2, h2, x1, w_up_c, cw_c, cb_c, w_down, g_f, ln_g, ln_b)


def _interleave_chunks(t, d_ff):
    lead = t.shape[:-1]
    t = t.reshape(lead + (2, d_ff // FFN_CHUNK, FFN_CHUNK))
    t = jnp.swapaxes(t, -3, -2)
    return t.reshape(lead + (2 * d_ff,))


def kernel(x, c, positions, w_ada, b_ada, w_in, b_fgate, gn_a, gn_b, w_out,
           ln1_g, ln1_b, w_up, conv_w, conv_b, w_down, ln2_g, ln2_b):
    B, S, D = x.shape
    depth = w_ada.shape[0]
    d_ff = w_down.shape[1]
    alpha = (2.0 * depth) ** 0.25
    scale = HEAD_DIM ** -0.5
    ts = 512
    tq = 256

    for l in range(depth):
        ada = _ada(c, w_ada[l], b_ada[l])
        sh_a, sc_a, g_a, sh_f, sc_f, g_f = (t[:, None, :] for t in jnp.split(ada, 6, axis=-1))

        w = w_in[l]
        o3 = 3 * WIDTH
        w_f = jnp.pad(w[:, o3:o3 + N_HEADS], ((0, 0), (0, F_PAD - N_HEADS)))
        ob = o3 + N_HEADS
        w_cat = jnp.concatenate(
            [w[:, 0:WIDTH] * scale, w[:, WIDTH:o3],
             w[:, ob:ob + WIDTH] * scale, w[:, ob + WIDTH:ob + 3 * WIDTH], w_f], axis=1).astype(BF16)

        qa, ka, va, qb, kb, vb, f_slab = _inproj(x, sc_a, sh_a, w_cat, positions, ts)

        b_pad = jnp.pad(b_fgate[l], (0, F_PAD - N_HEADS)).reshape(1, F_PAD)
        f_cum = _fcumsum(f_slab, b_pad)
        f_blocks = f_cum[:, :, :N_HEADS].transpose(0, 2, 1).reshape(B, N_HEADS, S // tq, 1, tq)

        oa = _fox(qa, ka, va, f_blocks, gn_a[l].reshape(1, WIDTH), tq)
        ob_ = _dilated(qb, kb, vb, gn_b[l].reshape(1, WIDTH))

        x1, h2 = _outproj(oa, ob_, x, w_out[l].astype(BF16), g_a,
                          ln1_g[l].reshape(1, D), ln1_b[l].reshape(1, D), sc_f, sh_f, alpha, ts)

        x = _ffn(h2, x1,
                 _interleave_chunks(w_up[l], d_ff).astype(BF16),
                 _interleave_chunks(conv_w[l], d_ff),
                 _interleave_chunks(conv_b[l], d_ff).reshape(1, 2 * d_ff),
                 w_down[l].astype(BF16), g_f,
                 ln2_g[l].reshape(1, D), ln2_b[l].reshape(1, D), alpha, ts)
    return x
```

```python
import functools

import jax
import jax.numpy as jnp
from jax import lax
from jax.experimental import pallas as pl
from jax.experimental.pallas import tpu as pltpu

F32 = jnp.float32
BF16 = jnp.bfloat16

HEAD_DIM = 64
N_HEADS = 8
WIDTH = N_HEADS * HEAD_DIM
LANES = 128
N_PAIRS = WIDTH // LANES
BLOCK = 128
DILATED_PATTERNS = ((128, 1), (512, 4), (2048, 16))
FOLD = 4
ROPE_THETA = 500000.0
ROPE_DIMS = HEAD_DIM // 4
ROPE_HALF = ROPE_DIMS // 2
CONV_WIDTH = 3
LN_EPS = 1e-5
RMS_EPS = 1e-6
NEG = -1e30
LOG2E = 1.4426950408889634
F_PAD = LANES
FFN_CHUNK = 256
HALO = 16
VMEM_LIMIT = 56 * 1024 * 1024


def _cparams(sem):
    return pltpu.CompilerParams(dimension_semantics=sem, vmem_limit_bytes=VMEM_LIMIT)


def _ada_kernel(c_ref, w_ref, b_ref, o_ref):
    c = c_ref[...]
    s = c / (1.0 + jnp.exp(-c))
    o_ref[...] = jnp.dot(s, w_ref[...], preferred_element_type=F32,
                         precision=lax.Precision.HIGHEST) + b_ref[...]


def _ada(c, w, b):
    B, D = c.shape
    N = w.shape[1]
    return pl.pallas_call(
        _ada_kernel,
        grid=(N // D,),
        in_specs=[pl.BlockSpec((B, D), lambda j: (0, 0)),
                  pl.BlockSpec((D, D), lambda j: (0, j)),
                  pl.BlockSpec((1, D), lambda j: (0, j))],
        out_specs=pl.BlockSpec((B, D), lambda j: (0, j)),
        out_shape=jax.ShapeDtypeStruct((B, N), F32),
        compiler_params=_cparams(("arbitrary",)),
        name="ada",
    )(c, w, b.reshape(1, N))


def _inproj_kernel(x_ref, sc_ref, sh_ref, w_ref, pos_ref, freq_ref,
                   qa_ref, ka_ref, va_ref, qb_ref, kb_ref, vb_ref, f_ref):
    ts = x_ref.shape[0]
    h = (x_ref[...] * (1.0 + sc_ref[...]) + sh_ref[...]).astype(BF16)

    def proj(g, width=WIDTH):
        return jnp.dot(h, w_ref[:, g * WIDTH:g * WIDTH + width], preferred_element_type=F32)

    qa_ref[...] = (proj(0) * LOG2E).astype(BF16)
    ka_ref[...] = proj(1).astype(BF16)
    va_ref[...] = proj(2).astype(BF16)
    vb_ref[...] = proj(5).astype(BF16)
    zf = proj(6, F_PAD)
    for j in range(ts // LANES):
        f_ref[:, j * LANES:(j + 1) * LANES] = zf[j * LANES:(j + 1) * LANES, :].T[0:N_HEADS, :]

    n_sub = ts // LANES
    p_rows = pos_ref[...]
    p_cols = jnp.concatenate([p_rows, jnp.zeros((8 - n_sub, LANES), F32)], axis=0).T
    ang = jnp.concatenate([p_cols[:, j:j + 1] * freq_ref[...] for j in range(n_sub)], axis=0)
    cosv = jnp.cos(ang)
    sinv = jnp.sin(ang)
    lane = lax.broadcasted_iota(jnp.int32, (ts, LANES), 1)
    first_half = (lane & (HEAD_DIM - 1)) < ROPE_HALF

    def rotary(t):
        nxt = pltpu.roll(t, LANES - ROPE_HALF, 1)
        prv = pltpu.roll(t, ROPE_HALF, 1)
        return t * cosv + jnp.where(first_half, -nxt, prv) * sinv

    for g, ref, gain in ((3, qb_ref, LOG2E), (4, kb_ref, None)):
        z = proj(g)
        for p in range(N_PAIRS):
            cs = slice(p * LANES, (p + 1) * LANES)
            r = rotary(z[:, cs])
            ref[:, cs] = (r if gain is None else r * gain).astype(BF16)


def _inproj(x, sc, sh, w_cat, positions, ts):
    B, S, D = x.shape
    n_cols = w_cat.shape[1]
    row = lambda b, i: (b, i, 0)
    per_b = lambda b, i: (b, 0, 0)
    wide = pl.BlockSpec((None, ts, WIDTH), row)
    pos = positions.astype(F32).reshape(B, S // ts, ts // LANES, LANES)
    freqs = ROPE_THETA ** (-jnp.arange(0, ROPE_DIMS, 2, dtype=F32) / ROPE_DIMS)
    head = jnp.concatenate([freqs, freqs, jnp.zeros((HEAD_DIM - ROPE_DIMS,), F32)])
    lane_freq = jnp.concatenate([head, head]).reshape(1, LANES)
    return pl.pallas_call(
        _inproj_kernel,
        grid=(B, S // ts),
        in_specs=[pl.BlockSpec((None, ts, D), row),
                  pl.BlockSpec((None, 1, D), per_b),
                  pl.BlockSpec((None, 1, D), per_b),
                  pl.BlockSpec((D, n_cols), lambda b, i: (0, 0)),
                  pl.BlockSpec((None, None, ts // LANES, LANES), lambda b, i: (b, i, 0, 0)),
                  pl.BlockSpec((1, LANES), lambda b, i: (0, 0))],
        out_specs=[wide] * 6 + [pl.BlockSpec((None, N_HEADS, ts), lambda b, i: (b, 0, i))],
        out_shape=[jax.ShapeDtypeStruct((B, S, WIDTH), BF16)] * 6
                  + [jax.ShapeDtypeStruct((B, N_HEADS, S), F32)],
        compiler_params=_cparams(("parallel", "parallel")),
        name="inproj",
    )(x, sc, sh, w_cat, pos, lane_freq)


def _fcumsum_kernel(f_ref, b_ref, o_ref):
    S = f_ref.shape[1]
    r = lax.broadcasted_iota(jnp.int32, (BLOCK, BLOCK), 0)
    c = lax.broadcasted_iota(jnp.int32, (BLOCK, BLOCK), 1)
    tri = jnp.where(r <= c, 1.0, 0.0).astype(BF16)

    def body(n, carry):
        cols = pl.ds(pl.multiple_of(n * BLOCK, BLOCK), BLOCK)
        z = f_ref[:, cols] + b_ref[...]
        logf = jnp.minimum(z, 0.0) - jnp.log(1.0 + jnp.exp(-jnp.abs(z)))
        hi = logf.astype(BF16)
        r1 = logf - hi.astype(F32)
        mid = r1.astype(BF16)
        lo = (r1 - mid.astype(F32)).astype(BF16)
        local = (jnp.dot(hi, tri, preferred_element_type=F32)
                 + jnp.dot(mid, tri, preferred_element_type=F32)
                 + jnp.dot(lo, tri, preferred_element_type=F32))
        tot = local + carry
        o_ref[:, cols] = tot * LOG2E
        return tot[:, BLOCK - 1:BLOCK]

    lax.fori_loop(0, S // BLOCK, body, jnp.zeros((f_ref.shape[0], 1), F32))


def _fcumsum(f_rows, b_rows):
    rows, S = f_rows.shape
    return pl.pallas_call(
        _fcumsum_kernel,
        in_specs=[pl.BlockSpec(memory_space=pltpu.VMEM)] * 2,
        out_specs=pl.BlockSpec(memory_space=pltpu.VMEM),
        out_shape=jax.ShapeDtypeStruct((rows, S), F32),
        compiler_params=pltpu.CompilerParams(vmem_limit_bytes=VMEM_LIMIT),
        name="fcumsum",
    )(f_rows, b_rows)


def _head_rms_gain(o, is_h0, gain):
    sq = o * o
    ms0 = jnp.sum(jnp.where(is_h0, sq, 0.0), axis=-1, keepdims=True) * (1.0 / HEAD_DIM)
    ms1 = jnp.sum(jnp.where(is_h0, 0.0, sq), axis=-1, keepdims=True) * (1.0 / HEAD_DIM)
    inv = jnp.where(is_h0, lax.rsqrt(ms0 + RMS_EPS), lax.rsqrt(ms1 + RMS_EPS))
    return o * inv * gain


_NT = (((1,), (1,)), ((), ()))


def _fox_kernel(q_ref, k_ref, v_ref, f_ref, gn_ref, o_ref, q_s, v_s, m_s, acc_s, *, tq):
    i = pl.program_id(1)
    S = k_ref.shape[0]
    lane = lax.broadcasted_iota(jnp.int32, (tq, LANES), 1)
    is_h0 = lane < HEAD_DIM
    own = (is_h0, jnp.logical_not(is_h0))
    den_lane = (HEAD_DIM, 0)

    @pl.when(i == 0)
    def _():
        def fill(n, _):
            rows = pl.ds(pl.multiple_of(n * tq, tq), tq)
            for p in range(N_PAIRS):
                vp = v_ref[rows, p * LANES:(p + 1) * LANES].astype(F32)
                for hh in range(2):
                    h = 2 * p + hh
                    ones = jnp.where(lane == den_lane[hh], 1.0, 0.0)
                    v_s[rows, h * LANES:(h + 1) * LANES] = jnp.where(own[hh], vp, ones).astype(BF16)
            return 0
        lax.fori_loop(0, S // tq, fill, 0)

    for p in range(N_PAIRS):
        qp = q_ref[:, p * LANES:(p + 1) * LANES].astype(F32)
        for hh in range(2):
            h = 2 * p + hh
            q_s[:, h * LANES:(h + 1) * LANES] = jnp.where(own[hh], qp, 0.0).astype(BF16)
    m_s[...] = jnp.full(m_s.shape, NEG, F32)
    acc_s[...] = jnp.zeros(acc_s.shape, F32)

    row = lax.broadcasted_iota(jnp.int32, (tq, tq), 0)
    col = lax.broadcasted_iota(jnp.int32, (tq, tq), 1)
    causal = col <= row
    fq = tuple(f_ref[h, i, :, 0:1] for h in range(N_HEADS))

    def tile(kb, masked):
        rows = pl.ds(pl.multiple_of(kb * tq, tq), tq)
        for h in range(N_HEADS):
            hs = slice(h * LANES, (h + 1) * LANES)
            p = h // 2
            k = k_ref[rows, p * LANES:(p + 1) * LANES]
            s = lax.dot_general(q_s[:, hs], k, _NT, preferred_element_type=F32)
            s = s + (fq[h] - f_ref[h, kb])
            if masked:
                s = jnp.where(causal, s, NEG)
            m_old = m_s[h]
            m_new = jnp.maximum(m_old, jnp.max(s, axis=-1, keepdims=True))
            alpha = jnp.exp2(m_old - m_new)
            pm = jnp.exp2(s - jnp.concatenate([m_new] * (tq // LANES), axis=1))
            pv = jnp.dot(pm.astype(BF16), v_s[rows, hs], preferred_element_type=F32)
            acc_s[h] = alpha * acc_s[h] + pv
            m_s[h] = m_new

    def body(j, _):
        for u in range(4):
            tile(4 * j + u, False)
        return 0

    lax.fori_loop(0, lax.shift_right_logical(i, 2), body, 0)
    left = i & 3
    for r in range(4):
        @pl.when(left == r)
        def _(r=r):
            for u in range(r):
                tile(i - r + u, False)
            tile(i, True)

    for p in range(N_PAIRS):
        o = []
        for hh in range(2):
            acc = acc_s[2 * p + hh]
            den = jnp.sum(jnp.where(lane == den_lane[hh], acc, 0.0), axis=-1, keepdims=True)
            o.append(acc / den)
        o = jnp.where(is_h0, o[0], o[1])
        cs = slice(p * LANES, (p + 1) * LANES)
        o_ref[:, cs] = _head_rms_gain(o, is_h0, gn_ref[:, cs]).astype(BF16)


def _fox(q, k, v, f_blocks, gn, tq):
    B, S, _ = q.shape
    full = pl.BlockSpec((None, S, WIDTH), lambda b, i: (b, 0, 0))
    return pl.pallas_call(
        functools.partial(_fox_kernel, tq=tq),
        grid=(B, S // tq),
        in_specs=[pl.BlockSpec((None, tq, WIDTH), lambda b, i: (b, i, 0)),
                  full, full,
                  pl.BlockSpec((None, N_HEADS, S // tq, 1, tq), lambda b, i: (b, 0, 0, 0, 0)),
                  pl.BlockSpec((1, WIDTH), lambda b, i: (0, 0))],
        out_specs=pl.BlockSpec((None, tq, WIDTH), lambda b, i: (b, i, 0)),
        out_shape=jax.ShapeDtypeStruct((B, S, WIDTH), BF16),
        scratch_shapes=[pltpu.VMEM((tq, N_HEADS * LANES), BF16),
                        pltpu.VMEM((S, N_HEADS * LANES), BF16),
                        pltpu.VMEM((N_HEADS, tq, LANES), F32),
                        pltpu.VMEM((N_HEADS, tq, LANES), F32)],
        compiler_params=_cparams(("parallel", "arbitrary")),
        name="fox",
    )(q, k, v, f_blocks, gn)


def _dil_kernel(q_ref, k_ref, v_ref, gn_ref, o_ref, nat, qL, kL, vL, accL, mL):
    S = q_ref.shape[0]
    R = S // FOLD
    chunk = 2 * BLOCK
    lane_c = lax.broadcasted_iota(jnp.int32, (chunk, LANES), 1)
    is_h0_c = lane_c < HEAD_DIM
    own = (is_h0_c, jnp.logical_not(is_h0_c))
    den_lane = (HEAD_DIM, 0)

    def fold(src_ref, store):
        def widen(n, _):
            rows = pl.ds(pl.multiple_of(n * chunk, chunk), chunk)
            nat[rows, :] = src_ref[rows, :].astype(F32)
            return 0

        lax.fori_loop(0, S // chunk, widen, 0)

        def spread(n, _):
            rows = pl.ds(pl.multiple_of(n * chunk, chunk), chunk)
            for j in range(FOLD):
                store(j, rows, nat[pl.ds(n * chunk * FOLD + j, chunk, stride=FOLD), :])
            return 0

        lax.fori_loop(0, R // chunk, spread, 0)

    def store_q(j, rows, x):
        for hh in range(2):
            qL[hh, j, rows, :] = jnp.where(own[hh], x, 0.0)

    def store_k(j, rows, x):
        kL[j, rows, :] = x

    def store_v(j, rows, x):
        for hh in range(2):
            vL[hh, j, rows, :] = jnp.where(own[hh], x, jnp.where(lane_c == den_lane[hh], 1.0, 0.0))

    fold(q_ref, store_q)
    fold(k_ref, store_k)
    fold(v_ref, store_v)

    def band_mask(nq, nk, off, steps, interleaved):
        qi = lax.broadcasted_iota(jnp.int32, (nq, nk), 0)
        kj = lax.broadcasted_iota(jnp.int32, (nq, nk), 1)
        if interleaved:
            sq, sk = nq // FOLD, nk // FOLD
            qi = FOLD * (qi % sq) + qi // sq
            kj = FOLD * (kj % sk) + kj // sk
        dist = qi + off - kj
        return (dist >= 0) & (dist <= steps)

    def tile_of(ref, lead, segs):
        parts = [ref[lead + (j, rows, slice(None))] for j, rows in segs]
        return parts[0] if len(parts) == 1 else jnp.concatenate(parts, axis=0)

    def blocks(descs, steps, first, interleaved):
        for q_segs, k_segs, off in descs:
            k = tile_of(kL, (), k_segs).astype(BF16)
            nk = k.shape[0]
            for hh in range(2):
                q = tile_of(qL, (hh,), q_segs).astype(BF16)
                nq = q.shape[0]
                seg = nq // len(q_segs)
                v = tile_of(vL, (hh,), k_segs).astype(BF16)
                s = lax.dot_general(q, k, _NT, preferred_element_type=F32)
                s = jnp.where(band_mask(nq, nk, off, steps, interleaved), s, NEG)
                m_new = jnp.broadcast_to(jnp.max(s, axis=-1, keepdims=True), (nq, LANES))
                if not first:
                    m_old = tile_of(mL, (hh,), q_segs)
                    m_new = jnp.maximum(m_old, m_new)
                    alpha = jnp.exp2(m_old - m_new)
                pm = jnp.exp2(s - jnp.concatenate([m_new] * (nk // LANES), axis=1))
                acc = jnp.dot(pm.astype(BF16), v, preferred_element_type=F32)
                if not first:
                    acc = acc + alpha * tile_of(accL, (hh,), q_segs)
                for n, (j, rows) in enumerate(q_segs):
                    accL[hh, j, rows, :] = acc[n * seg:(n + 1) * seg, :]
                    mL[hh, j, rows, :] = m_new[n * seg:(n + 1) * seg, :]

    wide = 2 * BLOCK
    group = 8
    for pi, (window, d) in enumerate(DILATED_PATTERNS[::-1]):
        steps = window // d
        nb = S // d // BLOCK
        first = pi == 0
        if d == 1:
            seg = BLOCK // FOLD

            def span(row0, rows):
                return [(j, pl.ds(row0, rows)) for j in range(FOLD)]

            opening = (span(0, 2 * seg), span(0, 2 * seg), 0)
            blocks([opening] + [(span(seg * n, seg), span(seg * (n - 1), 2 * seg), BLOCK)
                                for n in range(2, group)], steps, first, True)

            def body(g, _, steps=steps, first=first, span=span, seg=seg):
                base = pl.multiple_of(g * group * seg, group * seg)
                blocks([(span(base + seg * n, seg), span(base + seg * (n - 1), 2 * seg), BLOCK)
                        for n in range(group)], steps, first, True)
                return 0

            lax.fori_loop(1, nb // group, body, 0)
        elif d == FOLD:
            per = group // d
            blocks([([(j, pl.ds(0, wide))], [(j, pl.ds(0, wide))], 0) for j in range(d)],
                   steps, first, False)

            def body(g, _, steps=steps, first=first, per=per, d=d):
                descs = []
                for j in range(d):
                    for n in range(per):
                        start = pl.multiple_of((g * per + n) * BLOCK, BLOCK)
                        descs.append(([(j, pl.ds(start, BLOCK))],
                                      [(j, pl.ds(start - BLOCK, wide))], BLOCK))
                blocks(descs, steps, first, False)
                return 0

            lax.fori_loop(1, nb // per, body, 0)
        else:
            sub = d // FOLD
            assert nb * BLOCK == wide

            def body(jj, _, steps=steps, first=first, sub=sub):
                tile = lambda j: [(j, pl.ds(jj, wide, stride=sub))]
                blocks([(tile(j), tile(j), 0) for j in range(FOLD)], steps, first, False)
                return 0

            lax.fori_loop(0, sub, body, 0)

    def finish(n, _):
        rows = pl.ds(pl.multiple_of(n * chunk, chunk), chunk)
        for j in range(FOLD):
            o = []
            for hh in range(2):
                acc = accL[hh, j, rows, :]
                den = jnp.sum(jnp.where(lane_c == den_lane[hh], acc, 0.0), axis=-1, keepdims=True)
                o.append(acc / den)
            o = jnp.where(is_h0_c, o[0], o[1])
            nat[pl.ds(n * chunk * FOLD + j, chunk, stride=FOLD), :] = _head_rms_gain(
                o, is_h0_c, gn_ref[...])
        return 0

    lax.fori_loop(0, R // chunk, finish, 0)

    def narrow(n, _):
        rows = pl.ds(pl.multiple_of(n * chunk, chunk), chunk)
        o_ref[rows, :] = nat[rows, :].astype(BF16)
        return 0

    lax.fori_loop(0, S // chunk, narrow, 0)


def _dilated(q, k, v, gn):
    B, S, _ = q.shape
    blk = pl.BlockSpec((None, S, LANES), lambda b, p: (b, 0, p))
    folded = (FOLD, S // FOLD, LANES)
    pair = pltpu.VMEM((2,) + folded, F32)
    return pl.pallas_call(
        _dil_kernel,
        grid=(B, N_PAIRS),
        in_specs=[blk, blk, blk, pl.BlockSpec((1, LANES), lambda b, p: (0, p))],
        out_specs=blk,
        out_shape=jax.ShapeDtypeStruct((B, S, WIDTH), BF16),
        scratch_shapes=[pltpu.VMEM((S, LANES), F32), pair, pltpu.VMEM(folded, F32),
                        pair, pair, pair],
        compiler_params=_cparams(("parallel", "parallel")),
        name="dilated",
    )(q, k, v, gn)


def _layer_norm(x, g, b):
    mu = jnp.mean(x, axis=-1, keepdims=True)
    xc = x - mu
    var = jnp.mean(xc * xc, axis=-1, keepdims=True)
    return xc * lax.rsqrt(var + LN_EPS) * g + b


def _outffn_kernel(oa_ref, ob_ref, x_ref, wo_ref, ga_ref, l1g_ref, l1b_ref, sc_ref, sh_ref,
                   wup_ref, cw_ref, cb_ref, wdn_ref, gf_ref, l2g_ref, l2b_ref,
                   o_ref, hcat, u_s, act, x1_s, *, alpha, n_chunks):
    ts = x_ref.shape[0]
    i = pl.program_id(1)

    @pl.when(i == 0)
    def _():
        hcat[0:HALO, :] = jnp.zeros((HALO, hcat.shape[1]), BF16)

    @pl.when(i > 0)
    def _():
        hcat[0:HALO, :] = hcat[ts:ts + HALO, :]

    mix = (jnp.dot(oa_ref[...], wo_ref[0:WIDTH, :], preferred_element_type=F32)
           + jnp.dot(ob_ref[...], wo_ref[WIDTH:2 * WIDTH, :], preferred_element_type=F32))
    x1 = _layer_norm(alpha * x_ref[...] + ga_ref[...] * mix, l1g_ref[...], l1b_ref[...])
    x1_s[...] = x1
    hcat[HALO:, :] = (x1 * (1.0 + sc_ref[...]) + sh_ref[...]).astype(BF16)

    cc = 2 * FFN_CHUNK
    for c in range(n_chunks):
        cols = slice(c * cc, (c + 1) * cc)
        u = u_s.at[c % 2]
        u[...] = jnp.dot(hcat[...], wup_ref[:, cols], preferred_element_type=F32)
        cw = cw_ref[:, cols]
        y = cb_ref[:, cols]
        for t in range(CONV_WIDTH):
            off = HALO - (CONV_WIDTH - 1) + t
            y = y + u[off:off + ts, :] * cw[t:t + 1, :]
        a = y[:, :FFN_CHUNK]
        g = y[:, FFN_CHUNK:]
        act[:, c * FFN_CHUNK:(c + 1) * FFN_CHUNK] = (g / (1.0 + jnp.exp(-g)) * a).astype(BF16)
    ffn = jnp.dot(act[...], wdn_ref[...], preferred_element_type=F32)
    o_ref[...] = _layer_norm(alpha * x1_s[...] + gf_ref[...] * ffn, l2g_ref[...], l2b_ref[...])


def _outffn(oa, ob, x, w_out, g_a, ln1_g, ln1_b, sc_f, sh_f,
            w_up_c, cw_c, cb_c, w_down, g_f, ln2_g, ln2_b, alpha, ts):
    B, S, D = x.shape
    d_ff = w_down.shape[0]
    n_chunks = d_ff // FFN_CHUNK
    row = lambda b, i: (b, i, 0)
    const = lambda b, i: (0, 0)
    vec = pl.BlockSpec((1, D), const)
    vec_b = pl.BlockSpec((None, 1, D), lambda b, i: (b, 0, 0))
    once = pl.Buffered(1)
    return pl.pallas_call(
        functools.partial(_outffn_kernel, alpha=alpha, n_chunks=n_chunks),
        grid=(B, S // ts),
        in_specs=[pl.BlockSpec((None, ts, WIDTH), row),
                  pl.BlockSpec((None, ts, WIDTH), row),
                  pl.BlockSpec((None, ts, D), row),
                  pl.BlockSpec((2 * WIDTH, D), const, pipeline_mode=once),
                  vec_b, vec, vec, vec_b, vec_b,
                  pl.BlockSpec((D, 2 * d_ff), const, pipeline_mode=once),
                  pl.BlockSpec((CONV_WIDTH, 2 * d_ff), const),
                  pl.BlockSpec((1, 2 * d_ff), const),
                  pl.BlockSpec((d_ff, D), const, pipeline_mode=once),
                  vec_b, vec, vec],
        out_specs=pl.BlockSpec((None, ts, D), row),
        out_shape=jax.ShapeDtypeStruct((B, S, D), F32),
        scratch_shapes=[pltpu.VMEM((HALO + ts, D), BF16),
                        pltpu.VMEM((2, HALO + ts, 2 * FFN_CHUNK), F32),
                        pltpu.VMEM((ts, d_ff), BF16),
                        pltpu.VMEM((ts, D), F32)],
        compiler_params=_cparams(("parallel", "arbitrary")),
        name="outffn",
    )(oa, ob, x, w_out, g_a, ln1_g, ln1_b, sc_f, sh_f, w_up_c, cw_c, cb_c, w_down, g_f, ln2_g, ln2_b)


def _interleave_chunks(t, d_ff):
    lead = t.shape[:-1]
    t = t.reshape(lead + (2, d_ff // FFN_CHUNK, FFN_CHUNK))
    t = jnp.swapaxes(t, -3, -2)
    return t.reshape(lead + (2 * d_ff,))


def kernel(x, c, positions, w_ada, b_ada, w_in, b_fgate, gn_a, gn_b, w_out,
           ln1_g, ln1_b, w_up, conv_w, conv_b, w_down, ln2_g, ln2_b):
    B, S, D = x.shape
    depth = w_ada.shape[0]
    d_ff = w_down.shape[1]
    alpha = (2.0 * depth) ** 0.25
    scale = HEAD_DIM ** -0.5
    ts = 512
    tq = 256

    for l in range(depth):
        ada = _ada(c, w_ada[l], b_ada[l])
        sh_a, sc_a, g_a, sh_f, sc_f, g_f = (t[:, None, :] for t in jnp.split(ada, 6, axis=-1))

        w = w_in[l]
        o3 = 3 * WIDTH
        w_f = jnp.pad(w[:, o3:o3 + N_HEADS], ((0, 0), (0, F_PAD - N_HEADS)))
        ob = o3 + N_HEADS
        w_cat = jnp.concatenate(
            [w[:, 0:WIDTH] * scale, w[:, WIDTH:o3],
             w[:, ob:ob + WIDTH] * scale, w[:, ob + WIDTH:ob + 3 * WIDTH], w_f], axis=1).astype(BF16)

        qa, ka, va, qb, kb, vb, f_t = _inproj(x, sc_a, sh_a, w_cat, positions, ts)

        f_cum = _fcumsum(f_t.reshape(B * N_HEADS, S), jnp.tile(b_fgate[l], B).reshape(B * N_HEADS, 1))
        f_blocks = f_cum.reshape(B, N_HEADS, S // tq, 1, tq)

        oa = _fox(qa, ka, va, f_blocks, gn_a[l].reshape(1, WIDTH), tq)
        ob_ = _dilated(qb, kb, vb, gn_b[l].reshape(1, WIDTH))

        x = _outffn(oa, ob_, x, w_out[l].astype(BF16), g_a,
                    ln1_g[l].reshape(1, D), ln1_b[l].reshape(1, D), sc_f, sh_f,
                    _interleave_chunks(w_up[l], d_ff).astype(BF16),
                    _interleave_chunks(conv_w[l], d_ff),
                    _interleave_chunks(conv_b[l], d_ff).reshape(1, 2 * d_ff),
                    w_down[l].astype(BF16), g_f,
                    ln2_g[l].reshape(1, D), ln2_b[l].reshape(1, D), alpha, ts)
    return x
```

```python
import functools

import jax
import jax.numpy as jnp
from jax import lax
from jax.experimental import pallas as pl
from jax.experimental.pallas import tpu as pltpu

F32 = jnp.float32
BF16 = jnp.bfloat16

HEAD_DIM = 64
N_HEADS = 8
WIDTH = N_HEADS * HEAD_DIM
LANES = 128
N_PAIRS = WIDTH // LANES
BLOCK = 128
DILATED_PATTERNS = ((128, 1), (512, 4), (2048, 16))
FOLD = 4
ROPE_THETA = 500000.0
ROPE_DIMS = HEAD_DIM // 4
ROPE_HALF = ROPE_DIMS // 2
CONV_WIDTH = 3
LN_EPS = 1e-5
RMS_EPS = 1e-6
NEG = -1e30
LOG2E = 1.4426950408889634
F_PAD = LANES
FFN_CHUNK = 256
HALO = 16
VMEM_LIMIT = 56 * 1024 * 1024


def _cparams(sem):
    return pltpu.CompilerParams(dimension_semantics=sem, vmem_limit_bytes=VMEM_LIMIT)


def _ada_kernel(c_ref, w_ref, b_ref, o_ref):
    c = c_ref[...]
    s = c / (1.0 + jnp.exp(-c))
    o_ref[...] = jnp.dot(s, w_ref[...], preferred_element_type=F32,
                         precision=lax.Precision.HIGHEST) + b_ref[...]


def _ada(c, w, b):
    B, D = c.shape
    N = w.shape[1]
    return pl.pallas_call(
        _ada_kernel,
        grid=(N // D,),
        in_specs=[pl.BlockSpec((B, D), lambda j: (0, 0)),
                  pl.BlockSpec((D, D), lambda j: (0, j)),
                  pl.BlockSpec((1, D), lambda j: (0, j))],
        out_specs=pl.BlockSpec((B, D), lambda j: (0, j)),
        out_shape=jax.ShapeDtypeStruct((B, N), F32),
        compiler_params=_cparams(("arbitrary",)),
        name="ada",
    )(c, w, b.reshape(1, N))


def _inproj_kernel(x_ref, sc_ref, sh_ref, w_ref, pos_ref, freq_ref,
                   qa_ref, ka_ref, va_ref, qb_ref, kb_ref, vb_ref, f_ref):
    ts = x_ref.shape[0]
    h = (x_ref[...] * (1.0 + sc_ref[...]) + sh_ref[...]).astype(BF16)

    def proj(g, width=WIDTH):
        return jnp.dot(h, w_ref[:, g * WIDTH:g * WIDTH + width], preferred_element_type=F32)

    qa_ref[...] = (proj(0) * LOG2E).astype(BF16)
    ka_ref[...] = proj(1).astype(BF16)
    va_ref[...] = proj(2).astype(BF16)
    vb_ref[...] = proj(5).astype(BF16)
    zf = proj(6, F_PAD)
    for j in range(ts // LANES):
        f_ref[:, j * LANES:(j + 1) * LANES] = zf[j * LANES:(j + 1) * LANES, :].T[0:N_HEADS, :]

    n_sub = ts // LANES
    p_rows = pos_ref[...]
    p_cols = jnp.concatenate([p_rows, jnp.zeros((8 - n_sub, LANES), F32)], axis=0).T
    ang = jnp.concatenate([p_cols[:, j:j + 1] * freq_ref[...] for j in range(n_sub)], axis=0)
    cosv = jnp.cos(ang)
    sinv = jnp.sin(ang)
    lane = lax.broadcasted_iota(jnp.int32, (ts, LANES), 1)
    first_half = (lane & (HEAD_DIM - 1)) < ROPE_HALF

    def rotary(t):
        nxt = pltpu.roll(t, LANES - ROPE_HALF, 1)
        prv = pltpu.roll(t, ROPE_HALF, 1)
        return t * cosv + jnp.where(first_half, -nxt, prv) * sinv

    for g, ref, gain in ((3, qb_ref, LOG2E), (4, kb_ref, None)):
        z = proj(g)
        for p in range(N_PAIRS):
            cs = slice(p * LANES, (p + 1) * LANES)
            r = rotary(z[:, cs])
            ref[:, cs] = (r if gain is None else r * gain).astype(BF16)


def _inproj(x, sc, sh, w_cat, positions, ts):
    B, S, D = x.shape
    n_cols = w_cat.shape[1]
    row = lambda b, i: (b, i, 0)
    per_b = lambda b, i: (b, 0, 0)
    wide = pl.BlockSpec((None, ts, WIDTH), row)
    pos = positions.astype(F32).reshape(B, S // ts, ts // LANES, LANES)
    freqs = ROPE_THETA ** (-jnp.arange(0, ROPE_DIMS, 2, dtype=F32) / ROPE_DIMS)
    head = jnp.concatenate([freqs, freqs, jnp.zeros((HEAD_DIM - ROPE_DIMS,), F32)])
    lane_freq = jnp.concatenate([head, head]).reshape(1, LANES)
    return pl.pallas_call(
        _inproj_kernel,
        grid=(B, S // ts),
        in_specs=[pl.BlockSpec((None, ts, D), row),
                  pl.BlockSpec((None, 1, D), per_b),
                  pl.BlockSpec((None, 1, D), per_b),
                  pl.BlockSpec((D, n_cols), lambda b, i: (0, 0)),
                  pl.BlockSpec((None, None, ts // LANES, LANES), lambda b, i: (b, i, 0, 0)),
                  pl.BlockSpec((1, LANES), lambda b, i: (0, 0))],
        out_specs=[wide] * 6 + [pl.BlockSpec((None, N_HEADS, ts), lambda b, i: (b, 0, i))],
        out_shape=[jax.ShapeDtypeStruct((B, S, WIDTH), BF16)] * 6
                  + [jax.ShapeDtypeStruct((B, N_HEADS, S), F32)],
        compiler_params=_cparams(("parallel", "parallel")),
        name="inproj",
    )(x, sc, sh, w_cat, pos, lane_freq)


def _fcumsum_kernel(f_ref, b_ref, o_ref):
    S = f_ref.shape[1]
    r = lax.broadcasted_iota(jnp.int32, (BLOCK, BLOCK), 0)
    c = lax.broadcasted_iota(jnp.int32, (BLOCK, BLOCK), 1)
    tri = jnp.where(r <= c, 1.0, 0.0).astype(BF16)

    def body(n, carry):
        cols = pl.ds(pl.multiple_of(n * BLOCK, BLOCK), BLOCK)
        z = f_ref[:, cols] + b_ref[...]
        logf = jnp.minimum(z, 0.0) - jnp.log(1.0 + jnp.exp(-jnp.abs(z)))
        hi = logf.astype(BF16)
        r1 = logf - hi.astype(F32)
        mid = r1.astype(BF16)
        lo = (r1 - mid.astype(F32)).astype(BF16)
        local = (jnp.dot(hi, tri, preferred_element_type=F32)
                 + jnp.dot(mid, tri, preferred_element_type=F32)
                 + jnp.dot(lo, tri, preferred_element_type=F32))
        tot = local + carry
        o_ref[:, cols] = tot * LOG2E
        return tot[:, BLOCK - 1:BLOCK]

    lax.fori_loop(0, S // BLOCK, body, jnp.zeros((f_ref.shape[0], 1), F32))


def _fcumsum(f_rows, b_rows):
    rows, S = f_rows.shape
    return pl.pallas_call(
        _fcumsum_kernel,
        in_specs=[pl.BlockSpec(memory_space=pltpu.VMEM)] * 2,
        out_specs=pl.BlockSpec(memory_space=pltpu.VMEM),
        out_shape=jax.ShapeDtypeStruct((rows, S), F32),
        compiler_params=pltpu.CompilerParams(vmem_limit_bytes=VMEM_LIMIT),
        name="fcumsum",
    )(f_rows, b_rows)


def _head_rms_gain(o, is_h0, gain):
    sq = o * o
    ms0 = jnp.sum(jnp.where(is_h0, sq, 0.0), axis=-1, keepdims=True) * (1.0 / HEAD_DIM)
    ms1 = jnp.sum(jnp.where(is_h0, 0.0, sq), axis=-1, keepdims=True) * (1.0 / HEAD_DIM)
    inv = jnp.where(is_h0, lax.rsqrt(ms0 + RMS_EPS), lax.rsqrt(ms1 + RMS_EPS))
    return o * inv * gain


_NT = (((1,), (1,)), ((), ()))


def _fox_kernel(q_ref, k_ref, v_ref, f_ref, gn_ref, o_ref, q_s, v_s, m_s, acc_s, *, tq, tk):
    i = pl.program_id(1)
    S = k_ref.shape[0]
    per_q = tq // tk
    lane = lax.broadcasted_iota(jnp.int32, (tq, LANES), 1)
    is_h0 = lane < HEAD_DIM
    own = (is_h0, jnp.logical_not(is_h0))
    den_lane = (HEAD_DIM, 0)

    @pl.when(i == 0)
    def _():
        lane_k = lax.broadcasted_iota(jnp.int32, (tk, LANES), 1)

        def fill(n, _):
            rows = pl.ds(pl.multiple_of(n * tk, tk), tk)
            for p in range(N_PAIRS):
                vp = v_ref[rows, p * LANES:(p + 1) * LANES].astype(F32)
                for hh in range(2):
                    h = 2 * p + hh
                    ones = jnp.where(lane_k == den_lane[hh], 1.0, 0.0)
                    mine = (lane_k < HEAD_DIM) == (hh == 0)
                    v_s[rows, h * LANES:(h + 1) * LANES] = jnp.where(mine, vp, ones).astype(BF16)
            return 0
        lax.fori_loop(0, S // tk, fill, 0)

    for p in range(N_PAIRS):
        qp = q_ref[:, p * LANES:(p + 1) * LANES].astype(F32)
        for hh in range(2):
            h = 2 * p + hh
            q_s[:, h * LANES:(h + 1) * LANES] = jnp.where(own[hh], qp, 0.0).astype(BF16)
    m_s[...] = jnp.full(m_s.shape, NEG, F32)
    acc_s[...] = jnp.zeros(acc_s.shape, F32)

    row = lax.broadcasted_iota(jnp.int32, (tq, tk), 0)
    col = lax.broadcasted_iota(jnp.int32, (tq, tk), 1)
    fq = tuple(f_ref[h, i * per_q, :, 0:1] for h in range(N_HEADS))

    def tile(kb, ahead):
        rows = pl.ds(pl.multiple_of(kb * tk, tk), tk)
        for h in range(N_HEADS):
            hs = slice(h * LANES, (h + 1) * LANES)
            p = h // 2
            k = k_ref[rows, p * LANES:(p + 1) * LANES]
            s = lax.dot_general(q_s[:, hs], k, _NT, preferred_element_type=F32)
            s = s + (fq[h] - f_ref[h, kb])
            if ahead is not None:
                s = jnp.where(col + ahead <= row, s, NEG)
            m_old = m_s[h]
            m_new = jnp.maximum(m_old, jnp.max(s, axis=-1, keepdims=True))
            alpha = jnp.exp2(m_old - m_new)
            pm = jnp.exp2(s - jnp.concatenate([m_new] * (tk // LANES), axis=1))
            pv = jnp.dot(pm.astype(BF16), v_s[rows, hs], preferred_element_type=F32)
            acc_s[h] = alpha * acc_s[h] + pv
            m_s[h] = m_new

    def body(j, _):
        for u in range(per_q):
            tile(per_q * j + u, None)
        return 0

    lax.fori_loop(0, i, body, 0)
    for u in range(per_q):
        tile(per_q * i + u, u * tk)

    for p in range(N_PAIRS):
        o = []
        for hh in range(2):
            acc = acc_s[2 * p + hh]
            den = jnp.sum(jnp.where(lane == den_lane[hh], acc, 0.0), axis=-1, keepdims=True)
            o.append(acc / den)
        o = jnp.where(is_h0, o[0], o[1])
        cs = slice(p * LANES, (p + 1) * LANES)
        o_ref[:, cs] = _head_rms_gain(o, is_h0, gn_ref[:, cs]).astype(BF16)


def _fox(q, k, v, f_blocks, gn, tq, tk):
    B, S, _ = q.shape
    full = pl.BlockSpec((None, S, WIDTH), lambda b, i: (b, 0, 0))
    return pl.pallas_call(
        functools.partial(_fox_kernel, tq=tq, tk=tk),
        grid=(B, S // tq),
        in_specs=[pl.BlockSpec((None, tq, WIDTH), lambda b, i: (b, i, 0)),
                  full, full,
                  pl.BlockSpec((None, N_HEADS, S // tk, 1, tk), lambda b, i: (b, 0, 0, 0, 0)),
                  pl.BlockSpec((1, WIDTH), lambda b, i: (0, 0))],
        out_specs=pl.BlockSpec((None, tq, WIDTH), lambda b, i: (b, i, 0)),
        out_shape=jax.ShapeDtypeStruct((B, S, WIDTH), BF16),
        scratch_shapes=[pltpu.VMEM((tq, N_HEADS * LANES), BF16),
                        pltpu.VMEM((S, N_HEADS * LANES), BF16),
                        pltpu.VMEM((N_HEADS, tq, LANES), F32),
                        pltpu.VMEM((N_HEADS, tq, LANES), F32)],
        compiler_params=_cparams(("parallel", "arbitrary")),
        name="fox",
    )(q, k, v, f_blocks, gn)


def _dil_kernel(q_ref, k_ref, v_ref, gn_ref, o_ref, nat, qL, kL, vL, accL, mL):
    S = q_ref.shape[0]
    R = S // FOLD
    chunk = 2 * BLOCK
    lane_c = lax.broadcasted_iota(jnp.int32, (chunk, LANES), 1)
    is_h0_c = lane_c < HEAD_DIM
    own = (is_h0_c, jnp.logical_not(is_h0_c))
    den_lane = (HEAD_DIM, 0)

    def fold(src_ref, store):
        def widen(n, _):
            rows = pl.ds(pl.multiple_of(n * chunk, chunk), chunk)
            nat[rows, :] = src_ref[rows, :].astype(F32)
            return 0

        lax.fori_loop(0, S // chunk, widen, 0)

        def spread(n, _):
            rows = pl.ds(pl.multiple_of(n * chunk, chunk), chunk)
            for j in range(FOLD):
                store(j, rows, nat[pl.ds(n * chunk * FOLD + j, chunk, stride=FOLD), :])
            return 0

        lax.fori_loop(0, R // chunk, spread, 0)

    def store_q(j, rows, x):
        for hh in range(2):
            qL[hh, j, rows, :] = jnp.where(own[hh], x, 0.0)

    def store_k(j, rows, x):
        kL[j, rows, :] = x

    def store_v(j, rows, x):
        for hh in range(2):
            vL[hh, j, rows, :] = jnp.where(own[hh], x, jnp.where(lane_c == den_lane[hh], 1.0, 0.0))

    fold(q_ref, store_q)
    fold(k_ref, store_k)
    fold(v_ref, store_v)

    def band_mask(nq, nk, off, steps, interleaved):
        qi = lax.broadcasted_iota(jnp.int32, (nq, nk), 0)
        kj = lax.broadcasted_iota(jnp.int32, (nq, nk), 1)
        if interleaved:
            sq, sk = nq // FOLD, nk // FOLD
            qi = FOLD * (qi % sq) + qi // sq
            kj = FOLD * (kj % sk) + kj // sk
        dist = qi + off - kj
        return (dist >= 0) & (dist <= steps)

    def tile_of(ref, lead, segs):
        parts = [ref[lead + (j, rows, slice(None))] for j, rows in segs]
        return parts[0] if len(parts) == 1 else jnp.concatenate(parts, axis=0)

    def blocks(descs, steps, first, interleaved):
        for q_segs, k_segs, off in descs:
            k = tile_of(kL, (), k_segs).astype(BF16)
            nk = k.shape[0]
            for hh in range(2):
                q = tile_of(qL, (hh,), q_segs).astype(BF16)
                nq = q.shape[0]
                seg = nq // len(q_segs)
                v = tile_of(vL, (hh,), k_segs).astype(BF16)
                s = lax.dot_general(q, k, _NT, preferred_element_type=F32)
                s = jnp.where(band_mask(nq, nk, off, steps, interleaved), s, NEG)
                m_new = jnp.broadcast_to(jnp.max(s, axis=-1, keepdims=True), (nq, LANES))
                if not first:
                    m_old = tile_of(mL, (hh,), q_segs)
                    m_new = jnp.maximum(m_old, m_new)
                    alpha = jnp.exp2(m_old - m_new)
                pm = jnp.exp2(s - jnp.concatenate([m_new] * (nk // LANES), axis=1))
                acc = jnp.dot(pm.astype(BF16), v, preferred_element_type=F32)
                if not first:
                    acc = acc + alpha * tile_of(accL, (hh,), q_segs)
                for n, (j, rows) in enumerate(q_segs):
                    accL[hh, j, rows, :] = acc[n * seg:(n + 1) * seg, :]
                    mL[hh, j, rows, :] = m_new[n * seg:(n + 1) * seg, :]

    wide = 2 * BLOCK
    group = 8
    for pi, (window, d) in enumerate(DILATED_PATTERNS[::-1]):
        steps = window // d
        nb = S // d // BLOCK
        first = pi == 0
        if d == 1:
            seg = BLOCK // FOLD

            def span(row0, rows):
                return [(j, pl.ds(row0, rows)) for j in range(FOLD)]

            opening = (span(0, 2 * seg), span(0, 2 * seg), 0)
            blocks([opening] + [(span(seg * n, seg), span(seg * (n - 1), 2 * seg), BLOCK)
                                for n in range(2, group)], steps, first, True)

            def body(g, _, steps=steps, first=first, span=span, seg=seg):
                base = pl.multiple_of(g * group * seg, group * seg)
                blocks([(span(base + seg * n, seg), span(base + seg * (n - 1), 2 * seg), BLOCK)
                        for n in range(group)], steps, first, True)
                return 0

            lax.fori_loop(1, nb // group, body, 0)
        elif d == FOLD:
            per = group // d
            blocks([([(j, pl.ds(0, wide))], [(j, pl.ds(0, wide))], 0) for j in range(d)],
                   steps, first, False)

            def body(g, _, steps=steps, first=first, per=per, d=d):
                descs = []
                for j in range(d):
                    for n in range(per):
                        start = pl.multiple_of((g * per + n) * BLOCK, BLOCK)
                        descs.append(([(j, pl.ds(start, BLOCK))],
                                      [(j, pl.ds(start - BLOCK, wide))], BLOCK))
                blocks(descs, steps, first, False)
                return 0

            lax.fori_loop(1, nb // per, body, 0)
        else:
            sub = d // FOLD
            assert nb * BLOCK == wide

            def body(jj, _, steps=steps, first=first, sub=sub):
                tile = lambda j: [(j, pl.ds(jj, wide, stride=sub))]
                blocks([(tile(j), tile(j), 0) for j in range(FOLD)], steps, first, False)
                return 0

            lax.fori_loop(0, sub, body, 0)

    def finish(n, _):
        rows = pl.ds(pl.multiple_of(n * chunk, chunk), chunk)
        for j in range(FOLD):
            o = []
            for hh in range(2):
                acc = accL[hh, j, rows, :]
                den = jnp.sum(jnp.where(lane_c == den_lane[hh], acc, 0.0), axis=-1, keepdims=True)
                o.append(acc / den)
            o = jnp.where(is_h0_c, o[0], o[1])
            nat[pl.ds(n * chunk * FOLD + j, chunk, stride=FOLD), :] = _head_rms_gain(
                o, is_h0_c, gn_ref[...])
        return 0

    lax.fori_loop(0, R // chunk, finish, 0)

    def narrow(n, _):
        rows = pl.ds(pl.multiple_of(n * chunk, chunk), chunk)
        o_ref[rows, :] = nat[rows, :].astype(BF16)
        return 0

    lax.fori_loop(0, S // chunk, narrow, 0)


def _dilated(q, k, v, gn):
    B, S, _ = q.shape
    blk = pl.BlockSpec((None, S, LANES), lambda b, p: (b, 0, p))
    folded = (FOLD, S // FOLD, LANES)
    pair = pltpu.VMEM((2,) + folded, F32)
    return pl.pallas_call(
        _dil_kernel,
        grid=(B, N_PAIRS),
        in_specs=[blk, blk, blk, pl.BlockSpec((1, LANES), lambda b, p: (0, p))],
        out_specs=blk,
        out_shape=jax.ShapeDtypeStruct((B, S, WIDTH), BF16),
        scratch_shapes=[pltpu.VMEM((S, LANES), F32), pair, pltpu.VMEM(folded, F32),
                        pair, pair, pair],
        compiler_params=_cparams(("parallel", "parallel")),
        name="dilated",
    )(q, k, v, gn)


def _layer_norm(x, g, b):
    mu = jnp.mean(x, axis=-1, keepdims=True)
    xc = x - mu
    var = jnp.mean(xc * xc, axis=-1, keepdims=True)
    return xc * lax.rsqrt(var + LN_EPS) * g + b


def _outffn_kernel(oa_ref, ob_ref, x_ref, wo_ref, ga_ref, l1g_ref, l1b_ref, sc_ref, sh_ref,
                   wup_ref, cw_ref, cb_ref, wdn_ref, gf_ref, l2g_ref, l2b_ref,
                   o_ref, hcat, u_s, act, x1_s, *, alpha, n_chunks):
    ts = x_ref.shape[0]
    i = pl.program_id(1)

    @pl.when(i == 0)
    def _():
        hcat[0:HALO, :] = jnp.zeros((HALO, hcat.shape[1]), BF16)

    @pl.when(i > 0)
    def _():
        hcat[0:HALO, :] = hcat[ts:ts + HALO, :]

    mix = (jnp.dot(oa_ref[...], wo_ref[0:WIDTH, :], preferred_element_type=F32)
           + jnp.dot(ob_ref[...], wo_ref[WIDTH:2 * WIDTH, :], preferred_element_type=F32))
    x1 = _layer_norm(alpha * x_ref[...] + ga_ref[...] * mix, l1g_ref[...], l1b_ref[...])
    x1_s[...] = x1
    hcat[HALO:, :] = (x1 * (1.0 + sc_ref[...]) + sh_ref[...]).astype(BF16)

    d_ff = act.shape[1]
    for c in range(n_chunks):
        u = u_s.at[c % 2]
        halves = []
        for part in range(2):
            cols = slice(part * d_ff + c * FFN_CHUNK, part * d_ff + (c + 1) * FFN_CHUNK)
            u[part] = jnp.dot(hcat[...], wup_ref[:, cols], preferred_element_type=F32)
            cw = cw_ref[:, cols]
            y = cb_ref[:, cols]
            for t in range(CONV_WIDTH):
                off = HALO - (CONV_WIDTH - 1) + t
                y = y + u[part, off:off + ts, :] * cw[t:t + 1, :]
            halves.append(y)
        a, g = halves
        act[:, c * FFN_CHUNK:(c + 1) * FFN_CHUNK] = (g / (1.0 + jnp.exp(-g)) * a).astype(BF16)
    ffn = jnp.dot(act[...], wdn_ref[...], preferred_element_type=F32)
    o_ref[...] = _layer_norm(alpha * x1_s[...] + gf_ref[...] * ffn, l2g_ref[...], l2b_ref[...])


def _outffn(oa, ob, x, w_out, g_a, ln1_g, ln1_b, sc_f, sh_f,
            w_up, cw, cb, w_down, g_f, ln2_g, ln2_b, alpha, ts):
    B, S, D = x.shape
    d_ff = w_down.shape[0]
    n_chunks = d_ff // FFN_CHUNK
    row = lambda b, i: (b, i, 0)
    const = lambda b, i: (0, 0)
    vec = pl.BlockSpec((1, D), const)
    vec_b = pl.BlockSpec((None, 1, D), lambda b, i: (b, 0, 0))
    once = pl.Buffered(1)
    return pl.pallas_call(
        functools.partial(_outffn_kernel, alpha=alpha, n_chunks=n_chunks),
        grid=(B, S // ts),
        in_specs=[pl.BlockSpec((None, ts, WIDTH), row),
                  pl.BlockSpec((None, ts, WIDTH), row),
                  pl.BlockSpec((None, ts, D), row),
                  pl.BlockSpec((2 * WIDTH, D), const, pipeline_mode=once),
                  vec_b, vec, vec, vec_b, vec_b,
                  pl.BlockSpec((D, 2 * d_ff), const, pipeline_mode=once),
                  pl.BlockSpec((CONV_WIDTH, 2 * d_ff), const),
                  pl.BlockSpec((1, 2 * d_ff), const),
                  pl.BlockSpec((d_ff, D), const, pipeline_mode=once),
                  vec_b, vec, vec],
        out_specs=pl.BlockSpec((None, ts, D), row),
        out_shape=jax.ShapeDtypeStruct((B, S, D), F32),
        scratch_shapes=[pltpu.VMEM((HALO + ts, D), BF16),
                        pltpu.VMEM((2, 2, HALO + ts, FFN_CHUNK), F32),
                        pltpu.VMEM((ts, d_ff), BF16),
                        pltpu.VMEM((ts, D), F32)],
        compiler_params=_cparams(("parallel", "arbitrary")),
        name="outffn",
    )(oa, ob, x, w_out, g_a, ln1_g, ln1_b, sc_f, sh_f, w_up, cw, cb, w_down, g_f, ln2_g, ln2_b)


def kernel(x, c, positions, w_ada, b_ada, w_in, b_fgate, gn_a, gn_b, w_out,
           ln1_g, ln1_b, w_up, conv_w, conv_b, w_down, ln2_g, ln2_b):
    B, S, D = x.shape
    depth = w_ada.shape[0]
    d_ff = w_down.shape[1]
    alpha = (2.0 * depth) ** 0.25
    scale = HEAD_DIM ** -0.5
    ts = 512
    tq, tk = 512, 256

    for l in range(depth):
        ada = _ada(c, w_ada[l], b_ada[l])
        sh_a, sc_a, g_a, sh_f, sc_f, g_f = (t[:, None, :] for t in jnp.split(ada, 6, axis=-1))

        w = w_in[l]
        o3 = 3 * WIDTH
        w_f = jnp.pad(w[:, o3:o3 + N_HEADS], ((0, 0), (0, F_PAD - N_HEADS)))
        ob = o3 + N_HEADS
        w_cat = jnp.concatenate(
            [w[:, 0:WIDTH] * scale, w[:, WIDTH:o3],
             w[:, ob:ob + WIDTH] * scale, w[:, ob + WIDTH:ob + 3 * WIDTH], w_f], axis=1).astype(BF16)

        qa, ka, va, qb, kb, vb, f_t = _inproj(x, sc_a, sh_a, w_cat, positions, ts)

        f_cum = _fcumsum(f_t.reshape(B * N_HEADS, S), jnp.tile(b_fgate[l], B).reshape(B * N_HEADS, 1))
        f_blocks = f_cum.reshape(B, N_HEADS, S // tk, 1, tk)

        oa = _fox(qa, ka, va, f_blocks, gn_a[l].reshape(1, WIDTH), tq, tk)
        ob_ = _dilated(qb, kb, vb, gn_b[l].reshape(1, WIDTH))

        x = _outffn(oa, ob_, x, w_out[l].astype(BF16), g_a,
                    ln1_g[l].reshape(1, D), ln1_b[l].reshape(1, D), sc_f, sh_f,
                    w_up[l].astype(BF16), conv_w[l], conv_b[l].reshape(1, 2 * d_ff),
                    w_down[l].astype(BF16), g_f,
                    ln2_g[l].reshape(1, D), ln2_b[l].reshape(1, D), alpha, ts)
    return x
```

```python
import functools

import jax
import jax.numpy as jnp
from jax import lax
from jax.experimental import pallas as pl
from jax.experimental.pallas import tpu as pltpu

F32 = jnp.float32
BF16 = jnp.bfloat16

HEAD_DIM = 64
N_HEADS = 8
WIDTH = N_HEADS * HEAD_DIM
LANES = 128
N_PAIRS = WIDTH // LANES
BLOCK = 128
DILATED_PATTERNS = ((128, 1), (512, 4), (2048, 16))
FOLD = 4
ROPE_THETA = 500000.0
ROPE_DIMS = HEAD_DIM // 4
ROPE_HALF = ROPE_DIMS // 2
CONV_WIDTH = 3
LN_EPS = 1e-5
RMS_EPS = 1e-6
NEG = -1e30
LOG2E = 1.4426950408889634
F_PAD = LANES
FFN_CHUNK = 256
HALO = 16
VMEM_LIMIT = 56 * 1024 * 1024


def _cparams(sem):
    return pltpu.CompilerParams(dimension_semantics=sem, vmem_limit_bytes=VMEM_LIMIT)


def _ada_kernel(c_ref, w_ref, b_ref, o_ref):
    c = c_ref[...]
    s = c / (1.0 + jnp.exp(-c))
    o_ref[...] = jnp.dot(s, w_ref[...], preferred_element_type=F32,
                         precision=lax.Precision.HIGHEST) + b_ref[...]


def _ada(c, w, b):
    B, D = c.shape
    N = w.shape[1]
    return pl.pallas_call(
        _ada_kernel,
        grid=(N // D,),
        in_specs=[pl.BlockSpec((B, D), lambda j: (0, 0)),
                  pl.BlockSpec((D, D), lambda j: (0, j)),
                  pl.BlockSpec((1, D), lambda j: (0, j))],
        out_specs=pl.BlockSpec((B, D), lambda j: (0, j)),
        out_shape=jax.ShapeDtypeStruct((B, N), F32),
        compiler_params=_cparams(("arbitrary",)),
        name="ada",
    )(c, w, b.reshape(1, N))


def _inproj_kernel(x_ref, sc_ref, sh_ref, w_ref, pos_ref, freq_ref,
                   qa_ref, ka_ref, va_ref, qb_ref, kb_ref, vb_ref, f_ref):
    ts = x_ref.shape[0]
    h = (x_ref[...] * (1.0 + sc_ref[...]) + sh_ref[...]).astype(BF16)

    def proj(g, width=WIDTH):
        return jnp.dot(h, w_ref[:, g * WIDTH:g * WIDTH + width], preferred_element_type=F32)

    n_sub = ts // LANES
    p_rows = pos_ref[...]
    p_cols = jnp.concatenate([p_rows, jnp.zeros((8 - n_sub, LANES), F32)], axis=0).T
    ang = jnp.concatenate([p_cols[:, j:j + 1] * freq_ref[...] for j in range(n_sub)], axis=0)
    cosv = jnp.cos(ang)
    sinv = jnp.sin(ang)
    lane = lax.broadcasted_iota(jnp.int32, (ts, LANES), 1)
    first_half = (lane & (HEAD_DIM - 1)) < ROPE_HALF

    def rotary(t):
        nxt = pltpu.roll(t, LANES - ROPE_HALF, 1)
        prv = pltpu.roll(t, ROPE_HALF, 1)
        return t * cosv + jnp.where(first_half, -nxt, prv) * sinv

    for g, ref, gain in ((3, qb_ref, LOG2E), (4, kb_ref, None)):
        z = proj(g)
        for p in range(N_PAIRS):
            cs = slice(p * LANES, (p + 1) * LANES)
            r = rotary(z[:, cs])
            ref[:, cs] = (r if gain is None else r * gain).astype(BF16)

    zf = proj(6, F_PAD)
    for j in range(ts // LANES):
        f_ref[:, j * LANES:(j + 1) * LANES] = zf[j * LANES:(j + 1) * LANES, :].T[0:N_HEADS, :]
    qa_ref[...] = (proj(0) * LOG2E).astype(BF16)
    ka_ref[...] = proj(1).astype(BF16)
    va_ref[...] = proj(2).astype(BF16)
    vb_ref[...] = proj(5).astype(BF16)


def _inproj(x, sc, sh, w_cat, positions, ts):
    B, S, D = x.shape
    n_cols = w_cat.shape[1]
    row = lambda b, i: (b, i, 0)
    per_b = lambda b, i: (b, 0, 0)
    wide = pl.BlockSpec((None, ts, WIDTH), row)
    pos = positions.astype(F32).reshape(B, S // ts, ts // LANES, LANES)
    freqs = ROPE_THETA ** (-jnp.arange(0, ROPE_DIMS, 2, dtype=F32) / ROPE_DIMS)
    head = jnp.concatenate([freqs, freqs, jnp.zeros((HEAD_DIM - ROPE_DIMS,), F32)])
    lane_freq = jnp.concatenate([head, head]).reshape(1, LANES)
    return pl.pallas_call(
        _inproj_kernel,
        grid=(B, S // ts),
        in_specs=[pl.BlockSpec((None, ts, D), row),
                  pl.BlockSpec((None, 1, D), per_b),
                  pl.BlockSpec((None, 1, D), per_b),
                  pl.BlockSpec((D, n_cols), lambda b, i: (0, 0)),
                  pl.BlockSpec((None, None, ts // LANES, LANES), lambda b, i: (b, i, 0, 0)),
                  pl.BlockSpec((1, LANES), lambda b, i: (0, 0))],
        out_specs=[wide] * 6 + [pl.BlockSpec((None, N_HEADS, ts), lambda b, i: (b, 0, i))],
        out_shape=[jax.ShapeDtypeStruct((B, S, WIDTH), BF16)] * 6
                  + [jax.ShapeDtypeStruct((B, N_HEADS, S), F32)],
        compiler_params=_cparams(("parallel", "parallel")),
        name="inproj",
    )(x, sc, sh, w_cat, pos, lane_freq)


def _fcumsum_kernel(f_ref, b_ref, o_ref):
    S = f_ref.shape[1]
    r = lax.broadcasted_iota(jnp.int32, (BLOCK, BLOCK), 0)
    c = lax.broadcasted_iota(jnp.int32, (BLOCK, BLOCK), 1)
    tri = jnp.where(r <= c, 1.0, 0.0).astype(BF16)

    def body(n, carry):
        cols = pl.ds(pl.multiple_of(n * BLOCK, BLOCK), BLOCK)
        z = f_ref[:, cols] + b_ref[...]
        logf = jnp.minimum(z, 0.0) - jnp.log(1.0 + jnp.exp(-jnp.abs(z)))
        hi = logf.astype(BF16)
        r1 = logf - hi.astype(F32)
        mid = r1.astype(BF16)
        lo = (r1 - mid.astype(F32)).astype(BF16)
        local = (jnp.dot(hi, tri, preferred_element_type=F32)
                 + jnp.dot(mid, tri, preferred_element_type=F32)
                 + jnp.dot(lo, tri, preferred_element_type=F32))
        tot = local + carry
        o_ref[:, cols] = tot * LOG2E
        return tot[:, BLOCK - 1:BLOCK]

    lax.fori_loop(0, S // BLOCK, body, jnp.zeros((f_ref.shape[0], 1), F32))


def _fcumsum(f_rows, b_rows):
    rows, S = f_rows.shape
    return pl.pallas_call(
        _fcumsum_kernel,
        in_specs=[pl.BlockSpec(memory_space=pltpu.VMEM)] * 2,
        out_specs=pl.BlockSpec(memory_space=pltpu.VMEM),
        out_shape=jax.ShapeDtypeStruct((rows, S), F32),
        compiler_params=pltpu.CompilerParams(vmem_limit_bytes=VMEM_LIMIT),
        name="fcumsum",
    )(f_rows, b_rows)


def _head_rms_gain(o, is_h0, gain):
    sq = o * o
    ms0 = jnp.sum(jnp.where(is_h0, sq, 0.0), axis=-1, keepdims=True) * (1.0 / HEAD_DIM)
    ms1 = jnp.sum(jnp.where(is_h0, 0.0, sq), axis=-1, keepdims=True) * (1.0 / HEAD_DIM)
    inv = jnp.where(is_h0, lax.rsqrt(ms0 + RMS_EPS), lax.rsqrt(ms1 + RMS_EPS))
    return o * inv * gain


_NT = (((1,), (1,)), ((), ()))


def _fox_kernel(q_ref, k_ref, v_ref, f_ref, gn_ref, o_ref, q_s, v_s, m_s, acc_s, *, tq, tk):
    i = pl.program_id(1)
    S = k_ref.shape[0]
    per_q = tq // tk
    lane = lax.broadcasted_iota(jnp.int32, (tq, LANES), 1)
    is_h0 = lane < HEAD_DIM
    own = (is_h0, jnp.logical_not(is_h0))
    den_lane = (HEAD_DIM, 0)

    @pl.when(i == 0)
    def _():
        lane_k = lax.broadcasted_iota(jnp.int32, (tk, LANES), 1)

        def fill(n, _):
            rows = pl.ds(pl.multiple_of(n * tk, tk), tk)
            for p in range(N_PAIRS):
                vp = v_ref[rows, p * LANES:(p + 1) * LANES].astype(F32)
                for hh in range(2):
                    h = 2 * p + hh
                    ones = jnp.where(lane_k == den_lane[hh], 1.0, 0.0)
                    mine = (lane_k < HEAD_DIM) == (hh == 0)
                    v_s[rows, h * LANES:(h + 1) * LANES] = jnp.where(mine, vp, ones).astype(BF16)
            return 0
        lax.fori_loop(0, S // tk, fill, 0)

    for p in range(N_PAIRS):
        qp = q_ref[:, p * LANES:(p + 1) * LANES].astype(F32)
        for hh in range(2):
            h = 2 * p + hh
            q_s[:, h * LANES:(h + 1) * LANES] = jnp.where(own[hh], qp, 0.0).astype(BF16)
    m_s[...] = jnp.full(m_s.shape, NEG, F32)
    acc_s[...] = jnp.zeros(acc_s.shape, F32)

    fq = tuple(f_ref[h, i * per_q, :, 0:1] for h in range(N_HEADS))

    def tile(kb, ahead):
        rows = pl.ds(pl.multiple_of(kb * tk, tk), tk)
        qr = slice(0 if ahead is None else ahead, tq)
        nq = qr.stop - qr.start
        for h in range(N_HEADS):
            hs = slice(h * LANES, (h + 1) * LANES)
            p = h // 2
            k = k_ref[rows, p * LANES:(p + 1) * LANES]
            s = lax.dot_general(q_s[qr, hs], k, _NT, preferred_element_type=F32)
            s = s + (fq[h] - f_ref[h, kb])
            if ahead is not None:
                s = jnp.where(lax.broadcasted_iota(jnp.int32, (nq, tk), 1)
                              <= lax.broadcasted_iota(jnp.int32, (nq, tk), 0), s, NEG)
            m_old = m_s[h, qr, :]
            m_new = jnp.maximum(m_old, jnp.max(s, axis=-1, keepdims=True))
            alpha = jnp.exp2(m_old - m_new)
            pm = jnp.exp2(s - jnp.concatenate([m_new] * (tk // LANES), axis=1))
            pv = jnp.dot(pm.astype(BF16), v_s[rows, hs], preferred_element_type=F32)
            acc_s[h, qr, :] = alpha * acc_s[h, qr, :] + pv
            m_s[h, qr, :] = m_new

    def body(j, _):
        for u in range(per_q):
            tile(per_q * j + u, None)
        return 0

    lax.fori_loop(0, i, body, 0)
    for u in range(per_q):
        tile(per_q * i + u, u * tk)

    for p in range(N_PAIRS):
        o = []
        for hh in range(2):
            acc = acc_s[2 * p + hh]
            den = jnp.sum(jnp.where(lane == den_lane[hh], acc, 0.0), axis=-1, keepdims=True)
            o.append(acc / den)
        o = jnp.where(is_h0, o[0], o[1])
        cs = slice(p * LANES, (p + 1) * LANES)
        o_ref[:, cs] = _head_rms_gain(o, is_h0, gn_ref[:, cs]).astype(BF16)


def _fox(q, k, v, f_blocks, gn, tq, tk):
    B, S, _ = q.shape
    full = pl.BlockSpec((None, S, WIDTH), lambda b, i: (b, 0, 0))
    return pl.pallas_call(
        functools.partial(_fox_kernel, tq=tq, tk=tk),
        grid=(B, S // tq),
        in_specs=[pl.BlockSpec((None, tq, WIDTH), lambda b, i: (b, i, 0)),
                  full, full,
                  pl.BlockSpec((None, N_HEADS, S // tk, 1, tk), lambda b, i: (b, 0, 0, 0, 0)),
                  pl.BlockSpec((1, WIDTH), lambda b, i: (0, 0))],
        out_specs=pl.BlockSpec((None, tq, WIDTH), lambda b, i: (b, i, 0)),
        out_shape=jax.ShapeDtypeStruct((B, S, WIDTH), BF16),
        scratch_shapes=[pltpu.VMEM((tq, N_HEADS * LANES), BF16),
                        pltpu.VMEM((S, N_HEADS * LANES), BF16),
                        pltpu.VMEM((N_HEADS, tq, LANES), F32),
                        pltpu.VMEM((N_HEADS, tq, LANES), F32)],
        compiler_params=_cparams(("parallel", "arbitrary")),
        name="fox",
    )(q, k, v, f_blocks, gn)


def _dil_kernel(q_ref, k_ref, v_ref, gn_ref, o_ref, nat, qL, kL, vL, accL, mL):
    S = q_ref.shape[0]
    R = S // FOLD
    chunk = 2 * BLOCK
    lane_c = lax.broadcasted_iota(jnp.int32, (chunk, LANES), 1)
    is_h0_c = lane_c < HEAD_DIM
    own = (is_h0_c, jnp.logical_not(is_h0_c))
    den_lane = (HEAD_DIM, 0)

    def fold(src_ref, store):
        def widen(n, _):
            rows = pl.ds(pl.multiple_of(n * chunk, chunk), chunk)
            nat[rows, :] = src_ref[rows, :].astype(F32)
            return 0

        lax.fori_loop(0, S // chunk, widen, 0)

        def spread(n, _):
            rows = pl.ds(pl.multiple_of(n * chunk, chunk), chunk)
            for j in range(FOLD):
                store(j, rows, nat[pl.ds(n * chunk * FOLD + j, chunk, stride=FOLD), :])
            return 0

        lax.fori_loop(0, R // chunk, spread, 0)

    def store_q(j, rows, x):
        for hh in range(2):
            qL[hh, j, rows, :] = jnp.where(own[hh], x, 0.0)

    def store_k(j, rows, x):
        kL[j, rows, :] = x

    def store_v(j, rows, x):
        for hh in range(2):
            vL[hh, j, rows, :] = jnp.where(own[hh], x, jnp.where(lane_c == den_lane[hh], 1.0, 0.0))

    fold(q_ref, store_q)
    fold(k_ref, store_k)
    fold(v_ref, store_v)

    def band_mask(nq, nk, off, steps, interleaved):
        qi = lax.broadcasted_iota(jnp.int32, (nq, nk), 0)
        kj = lax.broadcasted_iota(jnp.int32, (nq, nk), 1)
        if interleaved:
            sq, sk = nq // FOLD, nk // FOLD
            qi = FOLD * (qi % sq) + qi // sq
            kj = FOLD * (kj % sk) + kj // sk
        dist = qi + off - kj
        return (dist >= 0) & (dist <= steps)

    def tile_of(ref, lead, segs):
        parts = [ref[lead + (j, rows, slice(None))] for j, rows in segs]
        return parts[0] if len(parts) == 1 else jnp.concatenate(parts, axis=0)

    def blocks(descs, steps, first, interleaved):
        for q_segs, k_segs, off in descs:
            k = tile_of(kL, (), k_segs).astype(BF16)
            nk = k.shape[0]
            for hh in range(2):
                q = tile_of(qL, (hh,), q_segs).astype(BF16)
                nq = q.shape[0]
                seg = nq // len(q_segs)
                v = tile_of(vL, (hh,), k_segs).astype(BF16)
                s = lax.dot_general(q, k, _NT, preferred_element_type=F32)
                s = jnp.where(band_mask(nq, nk, off, steps, interleaved), s, NEG)
                m_new = jnp.broadcast_to(jnp.max(s, axis=-1, keepdims=True), (nq, LANES))
                if not first:
                    m_old = tile_of(mL, (hh,), q_segs)
                    m_new = jnp.maximum(m_old, m_new)
                    alpha = jnp.exp2(m_old - m_new)
                pm = jnp.exp2(s - jnp.concatenate([m_new] * (nk // LANES), axis=1))
                acc = jnp.dot(pm.astype(BF16), v, preferred_element_type=F32)
                if not first:
                    acc = acc + alpha * tile_of(accL, (hh,), q_segs)
                for n, (j, rows) in enumerate(q_segs):
                    accL[hh, j, rows, :] = acc[n * seg:(n + 1) * seg, :]
                    mL[hh, j, rows, :] = m_new[n * seg:(n + 1) * seg, :]

    wide = 2 * BLOCK
    group = 8
    for pi, (window, d) in enumerate(DILATED_PATTERNS[::-1]):
        steps = window // d
        nb = S // d // BLOCK
        first = pi == 0
        if d == 1:
            seg = BLOCK // FOLD

            def span(row0, rows):
                return [(j, pl.ds(row0, rows)) for j in range(FOLD)]

            opening = (span(0, 2 * seg), span(0, 2 * seg), 0)
            blocks([opening] + [(span(seg * n, seg), span(seg * (n - 1), 2 * seg), BLOCK)
                                for n in range(2, group)], steps, first, True)

            def body(g, _, steps=steps, first=first, span=span, seg=seg):
                base = pl.multiple_of(g * group * seg, group * seg)
                blocks([(span(base + seg * n, seg), span(base + seg * (n - 1), 2 * seg), BLOCK)
                        for n in range(group)], steps, first, True)
                return 0

            lax.fori_loop(1, nb // group, body, 0)
        elif d == FOLD:
            per = group // d
            blocks([([(j, pl.ds(0, wide))], [(j, pl.ds(0, wide))], 0) for j in range(d)],
                   steps, first, False)

            def body(g, _, steps=steps, first=first, per=per, d=d):
                descs = []
                for j in range(d):
                    for n in range(per):
                        start = pl.multiple_of((g * per + n) * BLOCK, BLOCK)
                        descs.append(([(j, pl.ds(start, BLOCK))],
                                      [(j, pl.ds(start - BLOCK, wide))], BLOCK))
                blocks(descs, steps, first, False)
                return 0

            lax.fori_loop(1, nb // per, body, 0)
        else:
            sub = d // FOLD
            assert nb * BLOCK == wide

            def body(jj, _, steps=steps, first=first, sub=sub):
                tile = lambda j: [(j, pl.ds(jj, wide, stride=sub))]
                blocks([(tile(j), tile(j), 0) for j in range(FOLD)], steps, first, False)
                return 0

            lax.fori_loop(0, sub, body, 0)

    def finish(n, _):
        rows = pl.ds(pl.multiple_of(n * chunk, chunk), chunk)
        for j in range(FOLD):
            o = []
            for hh in range(2):
                acc = accL[hh, j, rows, :]
                den = jnp.sum(jnp.where(lane_c == den_lane[hh], acc, 0.0), axis=-1, keepdims=True)
                o.append(acc / den)
            o = jnp.where(is_h0_c, o[0], o[1])
            nat[pl.ds(n * chunk * FOLD + j, chunk, stride=FOLD), :] = _head_rms_gain(
                o, is_h0_c, gn_ref[...])
        return 0

    lax.fori_loop(0, R // chunk, finish, 0)

    def narrow(n, _):
        rows = pl.ds(pl.multiple_of(n * chunk, chunk), chunk)
        o_ref[rows, :] = nat[rows, :].astype(BF16)
        return 0

    lax.fori_loop(0, S // chunk, narrow, 0)


def _dilated(q, k, v, gn):
    B, S, _ = q.shape
    blk = pl.BlockSpec((None, S, LANES), lambda b, p: (b, 0, p))
    folded = (FOLD, S // FOLD, LANES)
    pair = pltpu.VMEM((2,) + folded, F32)
    return pl.pallas_call(
        _dil_kernel,
        grid=(B, N_PAIRS),
        in_specs=[blk, blk, blk, pl.BlockSpec((1, LANES), lambda b, p: (0, p))],
        out_specs=blk,
        out_shape=jax.ShapeDtypeStruct((B, S, WIDTH), BF16),
        scratch_shapes=[pltpu.VMEM((S, LANES), F32), pair, pltpu.VMEM(folded, F32),
                        pair, pair, pair],
        compiler_params=_cparams(("parallel", "parallel")),
        name="dilated",
    )(q, k, v, gn)


def _layer_norm(x, g, b):
    mu = jnp.mean(x, axis=-1, keepdims=True)
    xc = x - mu
    var = jnp.mean(xc * xc, axis=-1, keepdims=True)
    return xc * lax.rsqrt(var + LN_EPS) * g + b


def _outffn_kernel(oa_ref, ob_ref, x_ref, wo_ref, ga_ref, l1g_ref, l1b_ref, sc_ref, sh_ref,
                   wup_ref, cw_ref, cb_ref, wdn_ref, gf_ref, l2g_ref, l2b_ref,
                   o_ref, hcat, u_s, act, x1_s, *, alpha, n_chunks):
    ts = x_ref.shape[0]
    i = pl.program_id(1)

    @pl.when(i == 0)
    def _():
        hcat[0:HALO, :] = jnp.zeros((HALO, hcat.shape[1]), BF16)

    @pl.when(i > 0)
    def _():
        hcat[0:HALO, :] = hcat[ts:ts + HALO, :]

    mix = (jnp.dot(oa_ref[...], wo_ref[0:WIDTH, :], preferred_element_type=F32)
           + jnp.dot(ob_ref[...], wo_ref[WIDTH:2 * WIDTH, :], preferred_element_type=F32))
    x1 = _layer_norm(alpha * x_ref[...] + ga_ref[...] * mix, l1g_ref[...], l1b_ref[...])
    x1_s[...] = x1
    hcat[HALO:, :] = (x1 * (1.0 + sc_ref[...]) + sh_ref[...]).astype(BF16)

    d_ff = act.shape[1]
    for c in range(n_chunks):
        u = u_s.at[c % 2]
        halves = []
        for part in range(2):
            cols = slice(part * d_ff + c * FFN_CHUNK, part * d_ff + (c + 1) * FFN_CHUNK)
            u[part] = jnp.dot(hcat[...], wup_ref[:, cols], preferred_element_type=F32)
            cw = cw_ref[:, cols]
            y = cb_ref[:, cols]
            for t in range(CONV_WIDTH):
                off = HALO - (CONV_WIDTH - 1) + t
                y = y + u[part, off:off + ts, :] * cw[t:t + 1, :]
            halves.append(y)
        a, g = halves
        act[:, c * FFN_CHUNK:(c + 1) * FFN_CHUNK] = (g / (1.0 + jnp.exp(-g)) * a).astype(BF16)
    ffn = jnp.dot(act[...], wdn_ref[...], preferred_element_type=F32)
    o_ref[...] = _layer_norm(alpha * x1_s[...] + gf_ref[...] * ffn, l2g_ref[...], l2b_ref[...])


def _outffn(oa, ob, x, w_out, g_a, ln1_g, ln1_b, sc_f, sh_f,
            w_up, cw, cb, w_down, g_f, ln2_g, ln2_b, alpha, ts):
    B, S, D = x.shape
    d_ff = w_down.shape[0]
    n_chunks = d_ff // FFN_CHUNK
    row = lambda b, i: (b, i, 0)
    const = lambda b, i: (0, 0)
    vec = pl.BlockSpec((1, D), const)
    vec_b = pl.BlockSpec((None, 1, D), lambda b, i: (b, 0, 0))
    once = pl.Buffered(1)
    return pl.pallas_call(
        functools.partial(_outffn_kernel, alpha=alpha, n_chunks=n_chunks),
        grid=(B, S // ts),
        in_specs=[pl.BlockSpec((None, ts, WIDTH), row),
                  pl.BlockSpec((None, ts, WIDTH), row),
                  pl.BlockSpec((None, ts, D), row),
                  pl.BlockSpec((2 * WIDTH, D), const, pipeline_mode=once),
                  vec_b, vec, vec, vec_b, vec_b,
                  pl.BlockSpec((D, 2 * d_ff), const, pipeline_mode=once),
                  pl.BlockSpec((CONV_WIDTH, 2 * d_ff), const),
                  pl.BlockSpec((1, 2 * d_ff), const),
                  pl.BlockSpec((d_ff, D), const, pipeline_mode=once),
                  vec_b, vec, vec],
        out_specs=pl.BlockSpec((None, ts, D), row),
        out_shape=jax.ShapeDtypeStruct((B, S, D), F32),
        scratch_shapes=[pltpu.VMEM((HALO + ts, D), BF16),
                        pltpu.VMEM((2, 2, HALO + ts, FFN_CHUNK), F32),
                        pltpu.VMEM((ts, d_ff), BF16),
                        pltpu.VMEM((ts, D), F32)],
        compiler_params=_cparams(("parallel", "arbitrary")),
        name="outffn",
    )(oa, ob, x, w_out, g_a, ln1_g, ln1_b, sc_f, sh_f, w_up, cw, cb, w_down, g_f, ln2_g, ln2_b)


def kernel(x, c, positions, w_ada, b_ada, w_in, b_fgate, gn_a, gn_b, w_out,
           ln1_g, ln1_b, w_up, conv_w, conv_b, w_down, ln2_g, ln2_b):
    B, S, D = x.shape
    depth = w_ada.shape[0]
    d_ff = w_down.shape[1]
    alpha = (2.0 * depth) ** 0.25
    scale = HEAD_DIM ** -0.5
    ts = 512
    tq, tk = 512, 256

    for l in range(depth):
        ada = _ada(c, w_ada[l], b_ada[l])
        sh_a, sc_a, g_a, sh_f, sc_f, g_f = (t[:, None, :] for t in jnp.split(ada, 6, axis=-1))

        w = w_in[l]
        o3 = 3 * WIDTH
        w_f = jnp.pad(w[:, o3:o3 + N_HEADS], ((0, 0), (0, F_PAD - N_HEADS)))
        ob = o3 + N_HEADS
        w_cat = jnp.concatenate(
            [w[:, 0:WIDTH] * scale, w[:, WIDTH:o3],
             w[:, ob:ob + WIDTH] * scale, w[:, ob + WIDTH:ob + 3 * WIDTH], w_f], axis=1).astype(BF16)

        qa, ka, va, qb, kb, vb, f_t = _inproj(x, sc_a, sh_a, w_cat, positions, ts)

        f_cum = _fcumsum(f_t.reshape(B * N_HEADS, S), jnp.tile(b_fgate[l], B).reshape(B * N_HEADS, 1))
        f_blocks = f_cum.reshape(B, N_HEADS, S // tk, 1, tk)

        oa = _fox(qa, ka, va, f_blocks, gn_a[l].reshape(1, WIDTH), tq, tk)
        ob_ = _dilated(qb, kb, vb, gn_b[l].reshape(1, WIDTH))

        x = _outffn(oa, ob_, x, w_out[l].astype(BF16), g_a,
                    ln1_g[l].reshape(1, D), ln1_b[l].reshape(1, D), sc_f, sh_f,
                    w_up[l].astype(BF16), conv_w[l], conv_b[l].reshape(1, 2 * d_ff),
                    w_down[l].astype(BF16), g_f,
                    ln2_g[l].reshape(1, D), ln2_b[l].reshape(1, D), alpha, ts)
    return x
```

```python
import functools

import jax
import jax.numpy as jnp
from jax import lax
from jax.experimental import pallas as pl
from jax.experimental.pallas import tpu as pltpu

F32 = jnp.float32
BF16 = jnp.bfloat16

HEAD_DIM = 64
N_HEADS = 8
WIDTH = N_HEADS * HEAD_DIM
LANES = 128
N_PAIRS = WIDTH // LANES
BLOCK = 128
DILATED_PATTERNS = ((128, 1), (512, 4), (2048, 16))
FOLD = 4
ROPE_THETA = 500000.0
ROPE_DIMS = HEAD_DIM // 4
ROPE_HALF = ROPE_DIMS // 2
CONV_WIDTH = 3
LN_EPS = 1e-5
RMS_EPS = 1e-6
NEG = -1e30
LOG2E = 1.4426950408889634
F_PAD = LANES
FFN_CHUNK = 256
HALO = 16
VMEM_LIMIT = 56 * 1024 * 1024


def _cparams(sem):
    return pltpu.CompilerParams(dimension_semantics=sem, vmem_limit_bytes=VMEM_LIMIT)


def _ada_kernel(c_ref, w_ref, b_ref, o_ref):
    c = c_ref[...]
    s = c / (1.0 + jnp.exp(-c))
    o_ref[...] = jnp.dot(s, w_ref[...], preferred_element_type=F32,
                         precision=lax.Precision.HIGHEST) + b_ref[...]


def _ada(c, w, b):
    B, D = c.shape
    N = w.shape[1]
    return pl.pallas_call(
        _ada_kernel,
        grid=(N // D,),
        in_specs=[pl.BlockSpec((B, D), lambda j: (0, 0)),
                  pl.BlockSpec((D, D), lambda j: (0, j)),
                  pl.BlockSpec((1, D), lambda j: (0, j))],
        out_specs=pl.BlockSpec((B, D), lambda j: (0, j)),
        out_shape=jax.ShapeDtypeStruct((B, N), F32),
        compiler_params=_cparams(("arbitrary",)),
        name="ada",
    )(c, w, b.reshape(1, N))


def _inproj_kernel(x_ref, sc_ref, sh_ref, w_ref, pos_ref, freq_ref,
                   qa_ref, ka_ref, va_ref, qb_ref, kb_ref, vb_ref, f_ref):
    ts = x_ref.shape[0]
    h = (x_ref[...] * (1.0 + sc_ref[...]) + sh_ref[...]).astype(BF16)

    def proj(g, width=WIDTH):
        return jnp.dot(h, w_ref[:, g * WIDTH:g * WIDTH + width], preferred_element_type=F32)

    n_sub = ts // LANES
    p_rows = pos_ref[...]
    p_pad = p_rows if n_sub == 8 else jnp.concatenate([p_rows, jnp.zeros((8 - n_sub, LANES), F32)], axis=0)
    p_cols = p_pad.T
    ang = jnp.concatenate([p_cols[:, j:j + 1] * freq_ref[...] for j in range(n_sub)], axis=0)
    cosv = jnp.cos(ang)
    sinv = jnp.sin(ang)
    lane = lax.broadcasted_iota(jnp.int32, (ts, LANES), 1)
    first_half = (lane & (HEAD_DIM - 1)) < ROPE_HALF

    def rotary(t):
        nxt = pltpu.roll(t, LANES - ROPE_HALF, 1)
        prv = pltpu.roll(t, ROPE_HALF, 1)
        return t * cosv + jnp.where(first_half, -nxt, prv) * sinv

    for g, ref, gain in ((3, qb_ref, LOG2E), (4, kb_ref, None)):
        z = proj(g)
        for p in range(N_PAIRS):
            cs = slice(p * LANES, (p + 1) * LANES)
            r = rotary(z[:, cs])
            ref[:, cs] = (r if gain is None else r * gain).astype(BF16)

    zf = proj(6, F_PAD)
    for j in range(ts // LANES):
        f_ref[:, j * LANES:(j + 1) * LANES] = zf[j * LANES:(j + 1) * LANES, :].T[0:N_HEADS, :]
    qa_ref[...] = (proj(0) * LOG2E).astype(BF16)
    ka_ref[...] = proj(1).astype(BF16)
    va_ref[...] = proj(2).astype(BF16)
    vb_ref[...] = proj(5).astype(BF16)


def _inproj(x, sc, sh, w_cat, positions, ts):
    B, S, D = x.shape
    n_cols = w_cat.shape[1]
    row = lambda b, i: (b, i, 0)
    per_b = lambda b, i: (b, 0, 0)
    wide = pl.BlockSpec((None, ts, WIDTH), row)
    pos = positions.astype(F32).reshape(B, S // ts, ts // LANES, LANES)
    freqs = ROPE_THETA ** (-jnp.arange(0, ROPE_DIMS, 2, dtype=F32) / ROPE_DIMS)
    head = jnp.concatenate([freqs, freqs, jnp.zeros((HEAD_DIM - ROPE_DIMS,), F32)])
    lane_freq = jnp.concatenate([head, head]).reshape(1, LANES)
    return pl.pallas_call(
        _inproj_kernel,
        grid=(B, S // ts),
        in_specs=[pl.BlockSpec((None, ts, D), row),
                  pl.BlockSpec((None, 1, D), per_b),
                  pl.BlockSpec((None, 1, D), per_b),
                  pl.BlockSpec((D, n_cols), lambda b, i: (0, 0), pipeline_mode=pl.Buffered(1)),
                  pl.BlockSpec((None, None, ts // LANES, LANES), lambda b, i: (b, i, 0, 0)),
                  pl.BlockSpec((1, LANES), lambda b, i: (0, 0))],
        out_specs=[wide] * 6 + [pl.BlockSpec((None, N_HEADS, ts), lambda b, i: (b, 0, i))],
        out_shape=[jax.ShapeDtypeStruct((B, S, WIDTH), BF16)] * 6
                  + [jax.ShapeDtypeStruct((B, N_HEADS, S), F32)],
        compiler_params=_cparams(("parallel", "parallel")),
        name="inproj",
    )(x, sc, sh, w_cat, pos, lane_freq)


def _fcumsum_kernel(f_ref, b_ref, o_ref):
    S = f_ref.shape[1]
    r = lax.broadcasted_iota(jnp.int32, (BLOCK, BLOCK), 0)
    c = lax.broadcasted_iota(jnp.int32, (BLOCK, BLOCK), 1)
    tri = jnp.where(r <= c, 1.0, 0.0).astype(BF16)

    def body(n, carry):
        cols = pl.ds(pl.multiple_of(n * BLOCK, BLOCK), BLOCK)
        z = f_ref[:, cols] + b_ref[...]
        logf = jnp.minimum(z, 0.0) - jnp.log(1.0 + jnp.exp(-jnp.abs(z)))
        hi = logf.astype(BF16)
        r1 = logf - hi.astype(F32)
        mid = r1.astype(BF16)
        lo = (r1 - mid.astype(F32)).astype(BF16)
        local = (jnp.dot(hi, tri, preferred_element_type=F32)
                 + jnp.dot(mid, tri, preferred_element_type=F32)
                 + jnp.dot(lo, tri, preferred_element_type=F32))
        tot = local + carry
        o_ref[:, cols] = tot * LOG2E
        return tot[:, BLOCK - 1:BLOCK]

    lax.fori_loop(0, S // BLOCK, body, jnp.zeros((f_ref.shape[0], 1), F32))


def _fcumsum(f_rows, b_rows):
    rows, S = f_rows.shape
    return pl.pallas_call(
        _fcumsum_kernel,
        in_specs=[pl.BlockSpec(memory_space=pltpu.VMEM)] * 2,
        out_specs=pl.BlockSpec(memory_space=pltpu.VMEM),
        out_shape=jax.ShapeDtypeStruct((rows, S), F32),
        compiler_params=pltpu.CompilerParams(vmem_limit_bytes=VMEM_LIMIT),
        name="fcumsum",
    )(f_rows, b_rows)


def _pair_output(acc0, acc1, is_h0):
    num = jnp.where(is_h0, acc0, acc1)
    den = pltpu.roll(jnp.where(is_h0, acc1, acc0), HEAD_DIM, 1)
    return num / den


def _head_rms_gain(o, is_h0, gain):
    sq = o * o
    ms0 = jnp.sum(jnp.where(is_h0, sq, 0.0), axis=-1, keepdims=True) * (1.0 / HEAD_DIM)
    ms1 = jnp.sum(jnp.where(is_h0, 0.0, sq), axis=-1, keepdims=True) * (1.0 / HEAD_DIM)
    inv = jnp.where(is_h0, lax.rsqrt(ms0 + RMS_EPS), lax.rsqrt(ms1 + RMS_EPS))
    return o * inv * gain


_NT = (((1,), (1,)), ((), ()))


def _fox_kernel(q_ref, k_ref, v_ref, f_ref, gn_ref, o_ref, q_s, v_s, m_s, acc_s, *, tq, tk):
    i = pl.program_id(1)
    S = k_ref.shape[0]
    per_q = tq // tk
    lane = lax.broadcasted_iota(jnp.int32, (tq, LANES), 1)
    is_h0 = lane < HEAD_DIM
    own = (is_h0, jnp.logical_not(is_h0))
    den_lane = (HEAD_DIM, 0)

    @pl.when(i == 0)
    def _():
        lane_k = lax.broadcasted_iota(jnp.int32, (tk, LANES), 1)

        def fill(n, _):
            rows = pl.ds(pl.multiple_of(n * tk, tk), tk)
            for p in range(N_PAIRS):
                vp = v_ref[rows, p * LANES:(p + 1) * LANES].astype(F32)
                for hh in range(2):
                    h = 2 * p + hh
                    ones = jnp.where(lane_k == den_lane[hh], 1.0, 0.0)
                    mine = (lane_k < HEAD_DIM) == (hh == 0)
                    v_s[rows, h * LANES:(h + 1) * LANES] = jnp.where(mine, vp, ones).astype(BF16)
            return 0
        lax.fori_loop(0, S // tk, fill, 0)

    for p in range(N_PAIRS):
        qp = q_ref[:, p * LANES:(p + 1) * LANES].astype(F32)
        for hh in range(2):
            h = 2 * p + hh
            q_s[:, h * LANES:(h + 1) * LANES] = jnp.where(own[hh], qp, 0.0).astype(BF16)
    m_s[...] = jnp.full(m_s.shape, NEG, F32)
    acc_s[...] = jnp.zeros(acc_s.shape, F32)

    fq = tuple(f_ref[h, i * per_q, :, 0:1] for h in range(N_HEADS))

    def tile(kb, ahead):
        rows = pl.ds(pl.multiple_of(kb * tk, tk), tk)
        qr = slice(0 if ahead is None else ahead, tq)
        nq = qr.stop - qr.start
        for h in range(N_HEADS):
            hs = slice(h * LANES, (h + 1) * LANES)
            p = h // 2
            k = k_ref[rows, p * LANES:(p + 1) * LANES]
            s = lax.dot_general(q_s[qr, hs], k, _NT, preferred_element_type=F32)
            s = s + (fq[h] - f_ref[h, kb])
            if ahead is not None:
                s = jnp.where(lax.broadcasted_iota(jnp.int32, (nq, tk), 1)
                              <= lax.broadcasted_iota(jnp.int32, (nq, tk), 0), s, NEG)
            m_old = m_s[h, qr, :]
            m_new = jnp.maximum(m_old, jnp.max(s, axis=-1, keepdims=True))
            alpha = jnp.exp2(m_old - m_new)
            pm = jnp.exp2(s - jnp.concatenate([m_new] * (tk // LANES), axis=1))
            pv = jnp.dot(pm.astype(BF16), v_s[rows, hs], preferred_element_type=F32)
            acc_s[h, qr, :] = alpha * acc_s[h, qr, :] + pv
            m_s[h, qr, :] = m_new

    def body(j, _):
        for u in range(per_q):
            tile(per_q * j + u, None)
        return 0

    lax.fori_loop(0, i, body, 0)
    for u in range(per_q):
        tile(per_q * i + u, u * tk)

    for p in range(N_PAIRS):
        o = []
        for hh in range(2):
            acc = acc_s[2 * p + hh]
            den = jnp.sum(jnp.where(lane == den_lane[hh], acc, 0.0), axis=-1, keepdims=True)
            o.append(acc / den)
        o = jnp.where(is_h0, o[0], o[1])
        cs = slice(p * LANES, (p + 1) * LANES)
        o_ref[:, cs] = _head_rms_gain(o, is_h0, gn_ref[:, cs]).astype(BF16)


def _fox(q, k, v, f_blocks, gn, tq, tk):
    B, S, _ = q.shape
    full = pl.BlockSpec((None, S, WIDTH), lambda b, i: (b, 0, 0))
    return pl.pallas_call(
        functools.partial(_fox_kernel, tq=tq, tk=tk),
        grid=(B, S // tq),
        in_specs=[pl.BlockSpec((None, tq, WIDTH), lambda b, i: (b, i, 0)),
                  full, full,
                  pl.BlockSpec((None, N_HEADS, S // tk, 1, tk), lambda b, i: (b, 0, 0, 0, 0)),
                  pl.BlockSpec((1, WIDTH), lambda b, i: (0, 0))],
        out_specs=pl.BlockSpec((None, tq, WIDTH), lambda b, i: (b, i, 0)),
        out_shape=jax.ShapeDtypeStruct((B, S, WIDTH), BF16),
        scratch_shapes=[pltpu.VMEM((tq, N_HEADS * LANES), BF16),
                        pltpu.VMEM((S, N_HEADS * LANES), BF16),
                        pltpu.VMEM((N_HEADS, tq, LANES), F32),
                        pltpu.VMEM((N_HEADS, tq, LANES), F32)],
        compiler_params=_cparams(("parallel", "arbitrary")),
        name="fox",
    )(q, k, v, f_blocks, gn)


def _dil_kernel(q_ref, k_ref, v_ref, gn_ref, o_ref, nat, qL, kL, vL, accL, mL):
    S = q_ref.shape[0]
    R = S // FOLD
    chunk = 2 * BLOCK
    lane_c = lax.broadcasted_iota(jnp.int32, (chunk, LANES), 1)
    is_h0_c = lane_c < HEAD_DIM
    own = (is_h0_c, jnp.logical_not(is_h0_c))

    def fold(src_ref, store):
        def widen(n, _):
            rows = pl.ds(pl.multiple_of(n * chunk, chunk), chunk)
            nat[rows, :] = src_ref[rows, :].astype(F32)
            return 0

        lax.fori_loop(0, S // chunk, widen, 0)

        def spread(n, _):
            rows = pl.ds(pl.multiple_of(n * chunk, chunk), chunk)
            for j in range(FOLD):
                store(j, rows, nat[pl.ds(n * chunk * FOLD + j, chunk, stride=FOLD), :])
            return 0

        lax.fori_loop(0, R // chunk, spread, 0)

    def store_q(j, rows, x):
        for hh in range(2):
            qL[hh, j, rows, :] = jnp.where(own[hh], x, 0.0)

    def store_k(j, rows, x):
        kL[j, rows, :] = x

    def store_v(j, rows, x):
        for hh in range(2):
            vL[hh, j, rows, :] = jnp.where(own[hh], x, 1.0)

    fold(q_ref, store_q)
    fold(k_ref, store_k)
    fold(v_ref, store_v)

    def band_mask(nq, nk, off, steps, interleaved):
        qi = lax.broadcasted_iota(jnp.int32, (nq, nk), 0)
        kj = lax.broadcasted_iota(jnp.int32, (nq, nk), 1)
        if interleaved:
            sq, sk = nq // FOLD, nk // FOLD
            qi = FOLD * (qi % sq) + qi // sq
            kj = FOLD * (kj % sk) + kj // sk
        dist = qi + off - kj
        return (dist >= 0) & (dist <= steps)

    def tile_of(ref, lead, segs):
        parts = [ref[lead + (j, rows, slice(None))] for j, rows in segs]
        return parts[0] if len(parts) == 1 else jnp.concatenate(parts, axis=0)

    def blocks(descs, steps, first, interleaved):
        for q_segs, k_segs, off in descs:
            k = tile_of(kL, (), k_segs).astype(BF16)
            nk = k.shape[0]
            for hh in range(2):
                q = tile_of(qL, (hh,), q_segs).astype(BF16)
                nq = q.shape[0]
                seg = nq // len(q_segs)
                v = tile_of(vL, (hh,), k_segs).astype(BF16)
                s = lax.dot_general(q, k, _NT, preferred_element_type=F32)
                s = jnp.where(band_mask(nq, nk, off, steps, interleaved), s, NEG)
                m_new = jnp.broadcast_to(jnp.max(s, axis=-1, keepdims=True), (nq, LANES))
                if not first:
                    m_old = tile_of(mL, (hh,), q_segs)
                    m_new = jnp.maximum(m_old, m_new)
                    alpha = jnp.exp2(m_old - m_new)
                pm = jnp.exp2(s - jnp.concatenate([m_new] * (nk // LANES), axis=1))
                acc = jnp.dot(pm.astype(BF16), v, preferred_element_type=F32)
                if not first:
                    acc = acc + alpha * tile_of(accL, (hh,), q_segs)
                for n, (j, rows) in enumerate(q_segs):
                    accL[hh, j, rows, :] = acc[n * seg:(n + 1) * seg, :]
                    mL[hh, j, rows, :] = m_new[n * seg:(n + 1) * seg, :]

    wide = 2 * BLOCK
    group = 32
    for pi, (window, d) in enumerate(DILATED_PATTERNS[::-1]):
        steps = window // d
        nb = S // d // BLOCK
        first = pi == 0
        if d == 1:
            seg = BLOCK // FOLD

            def span(row0, rows):
                return [(j, pl.ds(row0, rows)) for j in range(FOLD)]

            opening = (span(0, 2 * seg), span(0, 2 * seg), 0)
            blocks([opening] + [(span(seg * n, seg), span(seg * (n - 1), 2 * seg), BLOCK)
                                for n in range(2, group)], steps, first, True)

            def body(g, _, steps=steps, first=first, span=span, seg=seg):
                base = pl.multiple_of(g * group * seg, group * seg)
                blocks([(span(base + seg * n, seg), span(base + seg * (n - 1), 2 * seg), BLOCK)
                        for n in range(group)], steps, first, True)
                return 0

            lax.fori_loop(1, nb // group, body, 0)
        elif d == FOLD:
            per = group // d
            blocks([([(j, pl.ds(0, wide))], [(j, pl.ds(0, wide))], 0) for j in range(d)]
                   + [([(j, pl.ds(n * BLOCK, BLOCK))], [(j, pl.ds((n - 1) * BLOCK, wide))], BLOCK)
                      for j in range(d) for n in range(2, per)], steps, first, False)

            def body(g, _, steps=steps, first=first, per=per, d=d):
                descs = []
                for j in range(d):
                    for n in range(per):
                        start = pl.multiple_of((g * per + n) * BLOCK, BLOCK)
                        descs.append(([(j, pl.ds(start, BLOCK))],
                                      [(j, pl.ds(start - BLOCK, wide))], BLOCK))
                blocks(descs, steps, first, False)
                return 0

            lax.fori_loop(1, nb // per, body, 0)
        else:
            sub = d // FOLD
            assert nb * BLOCK == wide

            tiles = group // (FOLD * nb)

            def body(g, _, steps=steps, first=first, sub=sub, tiles=tiles):
                tile = lambda j, jj: [(j, pl.ds(jj, wide, stride=sub))]
                blocks([(tile(j, g * tiles + t), tile(j, g * tiles + t), 0)
                        for t in range(tiles) for j in range(FOLD)], steps, first, False)
                return 0

            lax.fori_loop(0, sub // tiles, body, 0)

    def finish(n, _):
        rows = pl.ds(pl.multiple_of(n * chunk, chunk), chunk)
        for j in range(FOLD):
            o = _pair_output(accL[0, j, rows, :], accL[1, j, rows, :], is_h0_c)
            nat[pl.ds(n * chunk * FOLD + j, chunk, stride=FOLD), :] = _head_rms_gain(
                o, is_h0_c, gn_ref[...])
        return 0

    lax.fori_loop(0, R // chunk, finish, 0)

    def narrow(n, _):
        rows = pl.ds(pl.multiple_of(n * chunk, chunk), chunk)
        o_ref[rows, :] = nat[rows, :].astype(BF16)
        return 0

    lax.fori_loop(0, S // chunk, narrow, 0)


def _dilated(q, k, v, gn):
    B, S, _ = q.shape
    blk = pl.BlockSpec((None, S, LANES), lambda b, p: (b, 0, p))
    folded = (FOLD, S // FOLD, LANES)
    pair = pltpu.VMEM((2,) + folded, F32)
    return pl.pallas_call(
        _dil_kernel,
        grid=(B, N_PAIRS),
        in_specs=[blk, blk, blk, pl.BlockSpec((1, LANES), lambda b, p: (0, p))],
        out_specs=blk,
        out_shape=jax.ShapeDtypeStruct((B, S, WIDTH), BF16),
        scratch_shapes=[pltpu.VMEM((S, LANES), F32), pair, pltpu.VMEM(folded, F32),
                        pair, pair, pair],
        compiler_params=_cparams(("parallel", "parallel")),
        name="dilated",
    )(q, k, v, gn)


def _layer_norm(x, g, b):
    mu = jnp.mean(x, axis=-1, keepdims=True)
    xc = x - mu
    var = jnp.mean(xc * xc, axis=-1, keepdims=True)
    return xc * lax.rsqrt(var + LN_EPS) * g + b


def _outffn_kernel(oa_ref, ob_ref, x_ref, wo_ref, ga_ref, l1g_ref, l1b_ref, sc_ref, sh_ref,
                   wup_ref, cw_ref, cb_ref, wdn_ref, gf_ref, l2g_ref, l2b_ref,
                   o_ref, hcat, u_s, act, x1_s, *, alpha, n_chunks):
    ts = x_ref.shape[0]
    i = pl.program_id(1)

    @pl.when(i == 0)
    def _():
        hcat[0:HALO, :] = jnp.zeros((HALO, hcat.shape[1]), BF16)

    @pl.when(i > 0)
    def _():
        hcat[0:HALO, :] = hcat[ts:ts + HALO, :]

    mix = (jnp.dot(oa_ref[...], wo_ref[0:WIDTH, :], preferred_element_type=F32)
           + jnp.dot(ob_ref[...], wo_ref[WIDTH:2 * WIDTH, :], preferred_element_type=F32))
    x1 = _layer_norm(alpha * x_ref[...] + ga_ref[...] * mix, l1g_ref[...], l1b_ref[...])
    x1_s[...] = x1
    hcat[HALO:, :] = (x1 * (1.0 + sc_ref[...]) + sh_ref[...]).astype(BF16)

    d_ff = act.shape[1]
    for c in range(n_chunks):
        u = u_s.at[c % 2]
        halves = []
        for part in range(2):
            cols = slice(part * d_ff + c * FFN_CHUNK, part * d_ff + (c + 1) * FFN_CHUNK)
            u[part] = jnp.dot(hcat[...], wup_ref[:, cols], preferred_element_type=F32)
            cw = cw_ref[:, cols]
            y = cb_ref[:, cols]
            for t in range(CONV_WIDTH):
                off = HALO - (CONV_WIDTH - 1) + t
                y = y + u[part, off:off + ts, :] * cw[t:t + 1, :]
            halves.append(y)
        a, g = halves
        act[:, c * FFN_CHUNK:(c + 1) * FFN_CHUNK] = (g / (1.0 + jnp.exp(-g)) * a).astype(BF16)
    ffn = jnp.dot(act[...], wdn_ref[...], preferred_element_type=F32)
    o_ref[...] = _layer_norm(alpha * x1_s[...] + gf_ref[...] * ffn, l2g_ref[...], l2b_ref[...])


def _outffn(oa, ob, x, w_out, g_a, ln1_g, ln1_b, sc_f, sh_f,
            w_up, cw, cb, w_down, g_f, ln2_g, ln2_b, alpha, ts):
    B, S, D = x.shape
    d_ff = w_down.shape[0]
    n_chunks = d_ff // FFN_CHUNK
    row = lambda b, i: (b, i, 0)
    const = lambda b, i: (0, 0)
    vec = pl.BlockSpec((1, D), const)
    vec_b = pl.BlockSpec((None, 1, D), lambda b, i: (b, 0, 0))
    once = pl.Buffered(1)
    return pl.pallas_call(
        functools.partial(_outffn_kernel, alpha=alpha, n_chunks=n_chunks),
        grid=(B, S // ts),
        in_specs=[pl.BlockSpec((None, ts, WIDTH), row),
                  pl.BlockSpec((None, ts, WIDTH), row),
                  pl.BlockSpec((None, ts, D), row),
                  pl.BlockSpec((2 * WIDTH, D), const, pipeline_mode=once),
                  vec_b, vec, vec, vec_b, vec_b,
                  pl.BlockSpec((D, 2 * d_ff), const, pipeline_mode=once),
                  pl.BlockSpec((CONV_WIDTH, 2 * d_ff), const),
                  pl.BlockSpec((1, 2 * d_ff), const),
                  pl.BlockSpec((d_ff, D), const, pipeline_mode=once),
                  vec_b, vec, vec],
        out_specs=pl.BlockSpec((None, ts, D), row),
        out_shape=jax.ShapeDtypeStruct((B, S, D), F32),
        scratch_shapes=[pltpu.VMEM((HALO + ts, D), BF16),
                        pltpu.VMEM((2, 2, HALO + ts, FFN_CHUNK), F32),
                        pltpu.VMEM((ts, d_ff), BF16),
                        pltpu.VMEM((ts, D), F32)],
        compiler_params=_cparams(("parallel", "arbitrary")),
        name="outffn",
    )(oa, ob, x, w_out, g_a, ln1_g, ln1_b, sc_f, sh_f, w_up, cw, cb, w_down, g_f, ln2_g, ln2_b)


def kernel(x, c, positions, w_ada, b_ada, w_in, b_fgate, gn_a, gn_b, w_out,
           ln1_g, ln1_b, w_up, conv_w, conv_b, w_down, ln2_g, ln2_b):
    B, S, D = x.shape
    depth = w_ada.shape[0]
    d_ff = w_down.shape[1]
    alpha = (2.0 * depth) ** 0.25
    scale = HEAD_DIM ** -0.5
    ts = 512
    tq, tk = 512, 256

    for l in range(depth):
        ada = _ada(c, w_ada[l], b_ada[l])
        sh_a, sc_a, g_a, sh_f, sc_f, g_f = (t[:, None, :] for t in jnp.split(ada, 6, axis=-1))

        w = w_in[l]
        o3 = 3 * WIDTH
        w_f = jnp.pad(w[:, o3:o3 + N_HEADS], ((0, 0), (0, F_PAD - N_HEADS)))
        ob = o3 + N_HEADS
        w_cat = jnp.concatenate(
            [w[:, 0:WIDTH] * scale, w[:, WIDTH:o3],
             w[:, ob:ob + WIDTH] * scale, w[:, ob + WIDTH:ob + 3 * WIDTH], w_f], axis=1).astype(BF16)

        qa, ka, va, qb, kb, vb, f_t = _inproj(x, sc_a, sh_a, w_cat, positions, ts)

        f_cum = _fcumsum(f_t.reshape(B * N_HEADS, S), jnp.tile(b_fgate[l], B).reshape(B * N_HEADS, 1))
        f_blocks = f_cum.reshape(B, N_HEADS, S // tk, 1, tk)

        oa = _fox(qa, ka, va, f_blocks, gn_a[l].reshape(1, WIDTH), tq, tk)
        ob_ = _dilated(qb, kb, vb, gn_b[l].reshape(1, WIDTH))

        x = _outffn(oa, ob_, x, w_out[l].astype(BF16), g_a,
                    ln1_g[l].reshape(1, D), ln1_b[l].reshape(1, D), sc_f, sh_f,
                    w_up[l].astype(BF16), conv_w[l], conv_b[l].reshape(1, 2 * d_ff),
                    w_down[l].astype(BF16), g_f,
                    ln2_g[l].reshape(1, D), ln2_b[l].reshape(1, D), alpha, ts)
    return x
```

```python
import functools

import jax
import jax.numpy as jnp
from jax import lax
from jax.experimental import pallas as pl
from jax.experimental.pallas import tpu as pltpu

F32 = jnp.float32
BF16 = jnp.bfloat16

HEAD_DIM = 64
N_HEADS = 8
WIDTH = N_HEADS * HEAD_DIM
LANES = 128
N_PAIRS = WIDTH // LANES
BLOCK = 128
DILATED_PATTERNS = ((128, 1), (512, 4), (2048, 16))
FOLD = 4
ROPE_THETA = 500000.0
ROPE_DIMS = HEAD_DIM // 4
ROPE_HALF = ROPE_DIMS // 2
CONV_WIDTH = 3
LN_EPS = 1e-5
RMS_EPS = 1e-6
NEG = -1e30
LOG2E = 1.4426950408889634
F_PAD = LANES
FFN_CHUNK = 256
HALO = 16
VMEM_LIMIT = 56 * 1024 * 1024


def _cparams(sem):
    return pltpu.CompilerParams(dimension_semantics=sem, vmem_limit_bytes=VMEM_LIMIT)


def _ada_kernel(c_ref, w_ref, b_ref, o_ref):
    c = c_ref[...]
    s = c / (1.0 + jnp.exp(-c))
    o_ref[...] = jnp.dot(s, w_ref[...], preferred_element_type=F32,
                         precision=lax.Precision.HIGHEST) + b_ref[...]


def _ada(c, w, b):
    B, D = c.shape
    N = w.shape[1]
    return pl.pallas_call(
        _ada_kernel,
        grid=(N // D,),
        in_specs=[pl.BlockSpec((B, D), lambda j: (0, 0)),
                  pl.BlockSpec((D, D), lambda j: (0, j)),
                  pl.BlockSpec((1, D), lambda j: (0, j))],
        out_specs=pl.BlockSpec((B, D), lambda j: (0, j)),
        out_shape=jax.ShapeDtypeStruct((B, N), F32),
        compiler_params=_cparams(("arbitrary",)),
        name="ada",
    )(c, w, b.reshape(1, N))


def _inproj_kernel(x_ref, sc_ref, sh_ref, w_ref, pos_ref, freq_ref,
                   qa_ref, ka_ref, va_ref, qb_ref, kb_ref, vb_ref, f_ref):
    ts = x_ref.shape[0]
    h = (x_ref[...] * (1.0 + sc_ref[...]) + sh_ref[...]).astype(BF16)

    def proj(g, width=WIDTH):
        return jnp.dot(h, w_ref[:, g * WIDTH:g * WIDTH + width], preferred_element_type=F32)

    n_sub = ts // LANES
    p_rows = pos_ref[...]
    p_pad = p_rows if n_sub == 8 else jnp.concatenate([p_rows, jnp.zeros((8 - n_sub, LANES), F32)], axis=0)
    p_cols = p_pad.T
    ang = jnp.concatenate([p_cols[:, j:j + 1] * freq_ref[...] for j in range(n_sub)], axis=0)
    cosv = jnp.cos(ang)
    sinv = jnp.sin(ang)
    lane = lax.broadcasted_iota(jnp.int32, (ts, LANES), 1)
    first_half = (lane & (HEAD_DIM - 1)) < ROPE_HALF

    def rotary(t):
        nxt = pltpu.roll(t, LANES - ROPE_HALF, 1)
        prv = pltpu.roll(t, ROPE_HALF, 1)
        return t * cosv + jnp.where(first_half, -nxt, prv) * sinv

    for g, ref, gain in ((3, qb_ref, LOG2E), (4, kb_ref, None)):
        z = proj(g)
        for p in range(N_PAIRS):
            cs = slice(p * LANES, (p + 1) * LANES)
            r = rotary(z[:, cs])
            ref[:, cs] = (r if gain is None else r * gain).astype(BF16)

    zf = proj(6, F_PAD)
    for j in range(ts // LANES):
        f_ref[:, j * LANES:(j + 1) * LANES] = zf[j * LANES:(j + 1) * LANES, :].T[0:N_HEADS, :]
    qa_ref[...] = (proj(0) * LOG2E).astype(BF16)
    ka_ref[...] = proj(1).astype(BF16)
    va_ref[...] = proj(2).astype(BF16)
    vb_ref[...] = proj(5).astype(BF16)


def _inproj(x, sc, sh, w_cat, positions, ts):
    B, S, D = x.shape
    n_cols = w_cat.shape[1]
    row = lambda b, i: (b, i, 0)
    per_b = lambda b, i: (b, 0, 0)
    wide = pl.BlockSpec((None, ts, WIDTH), row)
    pos = positions.astype(F32).reshape(B, S // ts, ts // LANES, LANES)
    freqs = ROPE_THETA ** (-jnp.arange(0, ROPE_DIMS, 2, dtype=F32) / ROPE_DIMS)
    head = jnp.concatenate([freqs, freqs, jnp.zeros((HEAD_DIM - ROPE_DIMS,), F32)])
    lane_freq = jnp.concatenate([head, head]).reshape(1, LANES)
    return pl.pallas_call(
        _inproj_kernel,
        grid=(B, S // ts),
        in_specs=[pl.BlockSpec((None, ts, D), row),
                  pl.BlockSpec((None, 1, D), per_b),
                  pl.BlockSpec((None, 1, D), per_b),
                  pl.BlockSpec((D, n_cols), lambda b, i: (0, 0), pipeline_mode=pl.Buffered(1)),
                  pl.BlockSpec((None, None, ts // LANES, LANES), lambda b, i: (b, i, 0, 0)),
                  pl.BlockSpec((1, LANES), lambda b, i: (0, 0))],
        out_specs=[wide] * 6 + [pl.BlockSpec((None, N_HEADS, ts), lambda b, i: (b, 0, i))],
        out_shape=[jax.ShapeDtypeStruct((B, S, WIDTH), BF16)] * 6
                  + [jax.ShapeDtypeStruct((B, N_HEADS, S), F32)],
        compiler_params=_cparams(("parallel", "parallel")),
        name="inproj",
    )(x, sc, sh, w_cat, pos, lane_freq)


def _fcumsum_kernel(f_ref, b_ref, o_ref):
    S = f_ref.shape[1]
    r = lax.broadcasted_iota(jnp.int32, (BLOCK, BLOCK), 0)
    c = lax.broadcasted_iota(jnp.int32, (BLOCK, BLOCK), 1)
    tri = jnp.where(r <= c, 1.0, 0.0).astype(BF16)

    def body(n, carry):
        cols = pl.ds(pl.multiple_of(n * BLOCK, BLOCK), BLOCK)
        z = f_ref[:, cols] + b_ref[...]
        logf = jnp.minimum(z, 0.0) - jnp.log(1.0 + jnp.exp(-jnp.abs(z)))
        hi = logf.astype(BF16)
        r1 = logf - hi.astype(F32)
        mid = r1.astype(BF16)
        lo = (r1 - mid.astype(F32)).astype(BF16)
        local = (jnp.dot(hi, tri, preferred_element_type=F32)
                 + jnp.dot(mid, tri, preferred_element_type=F32)
                 + jnp.dot(lo, tri, preferred_element_type=F32))
        tot = local + carry
        o_ref[:, cols] = tot * LOG2E
        return tot[:, BLOCK - 1:BLOCK]

    lax.fori_loop(0, S // BLOCK, body, jnp.zeros((f_ref.shape[0], 1), F32))


def _fcumsum(f_rows, b_rows):
    rows, S = f_rows.shape
    return pl.pallas_call(
        _fcumsum_kernel,
        in_specs=[pl.BlockSpec(memory_space=pltpu.VMEM)] * 2,
        out_specs=pl.BlockSpec(memory_space=pltpu.VMEM),
        out_shape=jax.ShapeDtypeStruct((rows, S), F32),
        compiler_params=pltpu.CompilerParams(vmem_limit_bytes=VMEM_LIMIT),
        name="fcumsum",
    )(f_rows, b_rows)


def _pair_output(acc0, acc1, is_h0):
    num = jnp.where(is_h0, acc0, acc1)
    den = pltpu.roll(jnp.where(is_h0, acc1, acc0), HEAD_DIM, 1)
    return num / den


def _head_rms_gain(o, is_h0, gain):
    sq = o * o
    ms0 = jnp.sum(jnp.where(is_h0, sq, 0.0), axis=-1, keepdims=True) * (1.0 / HEAD_DIM)
    ms1 = jnp.sum(jnp.where(is_h0, 0.0, sq), axis=-1, keepdims=True) * (1.0 / HEAD_DIM)
    inv = jnp.where(is_h0, lax.rsqrt(ms0 + RMS_EPS), lax.rsqrt(ms1 + RMS_EPS))
    return o * inv * gain


_NT = (((1,), (1,)), ((), ()))


def _fox_kernel(q_ref, k_ref, v_ref, f_ref, gn_ref, o_ref, q_s, v_s, m_s, acc_s, *, tq, tk):
    i = pl.program_id(1)
    S = k_ref.shape[0]
    per_q = tq // tk
    lane = lax.broadcasted_iota(jnp.int32, (tq, LANES), 1)
    is_h0 = lane < HEAD_DIM
    own = (is_h0, jnp.logical_not(is_h0))
    den_lane = (HEAD_DIM, 0)

    @pl.when(i == 0)
    def _():
        lane_k = lax.broadcasted_iota(jnp.int32, (tk, LANES), 1)

        def fill(n, _):
            rows = pl.ds(pl.multiple_of(n * tk, tk), tk)
            for p in range(N_PAIRS):
                vp = v_ref[rows, p * LANES:(p + 1) * LANES].astype(F32)
                for hh in range(2):
                    h = 2 * p + hh
                    ones = jnp.where(lane_k == den_lane[hh], 1.0, 0.0)
                    mine = (lane_k < HEAD_DIM) == (hh == 0)
                    v_s[rows, h * LANES:(h + 1) * LANES] = jnp.where(mine, vp, ones).astype(BF16)
            return 0
        lax.fori_loop(0, S // tk, fill, 0)

    for p in range(N_PAIRS):
        qp = q_ref[:, p * LANES:(p + 1) * LANES].astype(F32)
        for hh in range(2):
            h = 2 * p + hh
            q_s[:, h * LANES:(h + 1) * LANES] = jnp.where(own[hh], qp, 0.0).astype(BF16)
    m_s[...] = jnp.full(m_s.shape, NEG, F32)
    acc_s[...] = jnp.zeros(acc_s.shape, F32)

    fq = tuple(f_ref[h, i * per_q, :, 0:1] for h in range(N_HEADS))

    def tile(kb, ahead):
        rows = pl.ds(pl.multiple_of(kb * tk, tk), tk)
        qr = slice(0 if ahead is None else ahead, tq)
        nq = qr.stop - qr.start
        for h in range(N_HEADS):
            hs = slice(h * LANES, (h + 1) * LANES)
            p = h // 2
            k = k_ref[rows, p * LANES:(p + 1) * LANES]
            s = lax.dot_general(q_s[qr, hs], k, _NT, preferred_element_type=F32)
            s = s + (fq[h] - f_ref[h, kb])
            if ahead is not None:
                s = jnp.where(lax.broadcasted_iota(jnp.int32, (nq, tk), 1)
                              <= lax.broadcasted_iota(jnp.int32, (nq, tk), 0), s, NEG)
            m_old = m_s[h, qr, :]
            m_new = jnp.maximum(m_old, jnp.max(s, axis=-1, keepdims=True))
            alpha = jnp.exp2(m_old - m_new)
            pm = jnp.exp2(s - jnp.concatenate([m_new] * (tk // LANES), axis=1))
            pv = jnp.dot(pm.astype(BF16), v_s[rows, hs], preferred_element_type=F32)
            acc_s[h, qr, :] = alpha * acc_s[h, qr, :] + pv
            m_s[h, qr, :] = m_new

    def body(j, _):
        for u in range(2 * per_q):
            tile(2 * per_q * j + u, None)
        return 0

    lax.fori_loop(0, lax.shift_right_logical(i, 1), body, 0)
    for r in range(2):
        @pl.when((i & 1) == r)
        def _(r=r):
            for u in range(r * per_q):
                tile(per_q * (i - r) + u, None)
            for u in range(per_q):
                tile(per_q * i + u, u * tk)

    for p in range(N_PAIRS):
        o = []
        for hh in range(2):
            acc = acc_s[2 * p + hh]
            den = jnp.sum(jnp.where(lane == den_lane[hh], acc, 0.0), axis=-1, keepdims=True)
            o.append(acc / den)
        o = jnp.where(is_h0, o[0], o[1])
        cs = slice(p * LANES, (p + 1) * LANES)
        o_ref[:, cs] = _head_rms_gain(o, is_h0, gn_ref[:, cs]).astype(BF16)


def _fox(q, k, v, f_blocks, gn, tq, tk):
    B, S, _ = q.shape
    full = pl.BlockSpec((None, S, WIDTH), lambda b, i: (b, 0, 0))
    return pl.pallas_call(
        functools.partial(_fox_kernel, tq=tq, tk=tk),
        grid=(B, S // tq),
        in_specs=[pl.BlockSpec((None, tq, WIDTH), lambda b, i: (b, i, 0)),
                  full, full,
                  pl.BlockSpec((None, N_HEADS, S // tk, 1, tk), lambda b, i: (b, 0, 0, 0, 0)),
                  pl.BlockSpec((1, WIDTH), lambda b, i: (0, 0))],
        out_specs=pl.BlockSpec((None, tq, WIDTH), lambda b, i: (b, i, 0)),
        out_shape=jax.ShapeDtypeStruct((B, S, WIDTH), BF16),
        scratch_shapes=[pltpu.VMEM((tq, N_HEADS * LANES), BF16),
                        pltpu.VMEM((S, N_HEADS * LANES), BF16),
                        pltpu.VMEM((N_HEADS, tq, LANES), F32),
                        pltpu.VMEM((N_HEADS, tq, LANES), F32)],
        compiler_params=_cparams(("parallel", "arbitrary")),
        name="fox",
    )(q, k, v, f_blocks, gn)


def _dil_kernel(q_ref, k_ref, v_ref, gn_ref, o_ref, nat, qL, kL, vL, accL, mL):
    S = q_ref.shape[0]
    R = S // FOLD
    chunk = 2 * BLOCK
    lane_c = lax.broadcasted_iota(jnp.int32, (chunk, LANES), 1)
    is_h0_c = lane_c < HEAD_DIM
    own = (is_h0_c, jnp.logical_not(is_h0_c))

    def fold(src_ref, store):
        def widen(n, _):
            rows = pl.ds(pl.multiple_of(n * chunk, chunk), chunk)
            nat[rows, :] = src_ref[rows, :].astype(F32)
            return 0

        lax.fori_loop(0, S // chunk, widen, 0)

        def spread(n, _):
            rows = pl.ds(pl.multiple_of(n * chunk, chunk), chunk)
            for j in range(FOLD):
                store(j, rows, nat[pl.ds(n * chunk * FOLD + j, chunk, stride=FOLD), :])
            return 0

        lax.fori_loop(0, R // chunk, spread, 0)

    def store_q(j, rows, x):
        for hh in range(2):
            qL[hh, j, rows, :] = jnp.where(own[hh], x, 0.0)

    def store_k(j, rows, x):
        kL[j, rows, :] = x

    def store_v(j, rows, x):
        for hh in range(2):
            vL[hh, j, rows, :] = jnp.where(own[hh], x, 1.0)

    fold(q_ref, store_q)
    fold(k_ref, store_k)
    fold(v_ref, store_v)

    def band_mask(nq, nk, off, steps, interleaved):
        qi = lax.broadcasted_iota(jnp.int32, (nq, nk), 0)
        kj = lax.broadcasted_iota(jnp.int32, (nq, nk), 1)
        if interleaved:
            sq, sk = nq // FOLD, nk // FOLD
            qi = FOLD * (qi % sq) + qi // sq
            kj = FOLD * (kj % sk) + kj // sk
        dist = qi + off - kj
        return (dist >= 0) & (dist <= steps)

    def tile_of(ref, lead, segs):
        parts = [ref[lead + (j, rows, slice(None))] for j, rows in segs]
        return parts[0] if len(parts) == 1 else jnp.concatenate(parts, axis=0)

    def blocks(descs, steps, first, interleaved):
        for q_segs, k_segs, off in descs:
            k = tile_of(kL, (), k_segs).astype(BF16)
            nk = k.shape[0]
            for hh in range(2):
                q = tile_of(qL, (hh,), q_segs).astype(BF16)
                nq = q.shape[0]
                seg = nq // len(q_segs)
                v = tile_of(vL, (hh,), k_segs).astype(BF16)
                s = lax.dot_general(q, k, _NT, preferred_element_type=F32)
                s = jnp.where(band_mask(nq, nk, off, steps, interleaved), s, NEG)
                m_new = jnp.broadcast_to(jnp.max(s, axis=-1, keepdims=True), (nq, LANES))
                if not first:
                    m_old = tile_of(mL, (hh,), q_segs)
                    m_new = jnp.maximum(m_old, m_new)
                    alpha = jnp.exp2(m_old - m_new)
                pm = jnp.exp2(s - jnp.concatenate([m_new] * (nk // LANES), axis=1))
                acc = jnp.dot(pm.astype(BF16), v, preferred_element_type=F32)
                if not first:
                    acc = acc + alpha * tile_of(accL, (hh,), q_segs)
                for n, (j, rows) in enumerate(q_segs):
                    accL[hh, j, rows, :] = acc[n * seg:(n + 1) * seg, :]
                    mL[hh, j, rows, :] = m_new[n * seg:(n + 1) * seg, :]

    wide = 2 * BLOCK
    group = 32
    for pi, (window, d) in enumerate(DILATED_PATTERNS[::-1]):
        steps = window // d
        nb = S // d // BLOCK
        first = pi == 0
        if d == 1:
            seg = BLOCK // FOLD

            def span(row0, rows):
                return [(j, pl.ds(row0, rows)) for j in range(FOLD)]

            opening = (span(0, 2 * seg), span(0, 2 * seg), 0)
            blocks([opening] + [(span(seg * n, seg), span(seg * (n - 1), 2 * seg), BLOCK)
                                for n in range(2, group)], steps, first, True)

            def body(g, _, steps=steps, first=first, span=span, seg=seg):
                base = pl.multiple_of(g * group * seg, group * seg)
                blocks([(span(base + seg * n, seg), span(base + seg * (n - 1), 2 * seg), BLOCK)
                        for n in range(group)], steps, first, True)
                return 0

            lax.fori_loop(1, nb // group, body, 0)
        elif d == FOLD:
            per = group // d
            blocks([([(j, pl.ds(0, wide))], [(j, pl.ds(0, wide))], 0) for j in range(d)]
                   + [([(j, pl.ds(n * BLOCK, BLOCK))], [(j, pl.ds((n - 1) * BLOCK, wide))], BLOCK)
                      for j in range(d) for n in range(2, per)], steps, first, False)

            def body(g, _, steps=steps, first=first, per=per, d=d):
                descs = []
                for j in range(d):
                    for n in range(per):
                        start = pl.multiple_of((g * per + n) * BLOCK, BLOCK)
                        descs.append(([(j, pl.ds(start, BLOCK))],
                                      [(j, pl.ds(start - BLOCK, wide))], BLOCK))
                blocks(descs, steps, first, False)
                return 0

            lax.fori_loop(1, nb // per, body, 0)
        else:
            sub = d // FOLD
            assert nb * BLOCK == wide

            tiles = group // (FOLD * nb)

            def body(g, _, steps=steps, first=first, sub=sub, tiles=tiles):
                tile = lambda j, jj: [(j, pl.ds(jj, wide, stride=sub))]
                blocks([(tile(j, g * tiles + t), tile(j, g * tiles + t), 0)
                        for t in range(tiles) for j in range(FOLD)], steps, first, False)
                return 0

            lax.fori_loop(0, sub // tiles, body, 0)

    def finish(n, _):
        rows = pl.ds(pl.multiple_of(n * chunk, chunk), chunk)
        for j in range(FOLD):
            o = _pair_output(accL[0, j, rows, :], accL[1, j, rows, :], is_h0_c)
            nat[pl.ds(n * chunk * FOLD + j, chunk, stride=FOLD), :] = _head_rms_gain(
                o, is_h0_c, gn_ref[...])
        return 0

    lax.fori_loop(0, R // chunk, finish, 0)

    def narrow(n, _):
        rows = pl.ds(pl.multiple_of(n * chunk, chunk), chunk)
        o_ref[rows, :] = nat[rows, :].astype(BF16)
        return 0

    lax.fori_loop(0, S // chunk, narrow, 0)


def _dilated(q, k, v, gn):
    B, S, _ = q.shape
    blk = pl.BlockSpec((None, S, LANES), lambda b, p: (b, 0, p))
    folded = (FOLD, S // FOLD, LANES)
    pair = pltpu.VMEM((2,) + folded, F32)
    return pl.pallas_call(
        _dil_kernel,
        grid=(B, N_PAIRS),
        in_specs=[blk, blk, blk, pl.BlockSpec((1, LANES), lambda b, p: (0, p))],
        out_specs=blk,
        out_shape=jax.ShapeDtypeStruct((B, S, WIDTH), BF16),
        scratch_shapes=[pltpu.VMEM((S, LANES), F32), pair, pltpu.VMEM(folded, F32),
                        pair, pair, pair],
        compiler_params=_cparams(("parallel", "parallel")),
        name="dilated",
    )(q, k, v, gn)


def _layer_norm(x, g, b):
    mu = jnp.mean(x, axis=-1, keepdims=True)
    xc = x - mu
    var = jnp.mean(xc * xc, axis=-1, keepdims=True)
    return xc * lax.rsqrt(var + LN_EPS) * g + b


def _outffn_kernel(oa_ref, ob_ref, x_ref, wo_ref, ga_ref, l1g_ref, l1b_ref, sc_ref, sh_ref,
                   wup_ref, cw_ref, cb_ref, wdn_ref, gf_ref, l2g_ref, l2b_ref,
                   o_ref, hcat, u_s, act, x1_s, *, alpha, n_chunks):
    ts = x_ref.shape[0]
    i = pl.program_id(1)

    @pl.when(i == 0)
    def _():
        hcat[0:HALO, :] = jnp.zeros((HALO, hcat.shape[1]), BF16)

    @pl.when(i > 0)
    def _():
        hcat[0:HALO, :] = hcat[ts:ts + HALO, :]

    mix = (jnp.dot(oa_ref[...], wo_ref[0:WIDTH, :], preferred_element_type=F32)
           + jnp.dot(ob_ref[...], wo_ref[WIDTH:2 * WIDTH, :], preferred_element_type=F32))
    x1 = _layer_norm(alpha * x_ref[...] + ga_ref[...] * mix, l1g_ref[...], l1b_ref[...])
    x1_s[...] = x1
    hcat[HALO:, :] = (x1 * (1.0 + sc_ref[...]) + sh_ref[...]).astype(BF16)

    d_ff = act.shape[1]
    for c in range(n_chunks):
        u = u_s.at[c % 2]
        halves = []
        for part in range(2):
            cols = slice(part * d_ff + c * FFN_CHUNK, part * d_ff + (c + 1) * FFN_CHUNK)
            u[part] = jnp.dot(hcat[...], wup_ref[:, cols], preferred_element_type=F32)
            cw = cw_ref[:, cols]
            y = cb_ref[:, cols]
            for t in range(CONV_WIDTH):
                off = HALO - (CONV_WIDTH - 1) + t
                y = y + u[part, off:off + ts, :] * cw[t:t + 1, :]
            halves.append(y)
        a, g = halves
        act[:, c * FFN_CHUNK:(c + 1) * FFN_CHUNK] = (g / (1.0 + jnp.exp(-g)) * a).astype(BF16)
    ffn = jnp.dot(act[...], wdn_ref[...], preferred_element_type=F32)
    o_ref[...] = _layer_norm(alpha * x1_s[...] + gf_ref[...] * ffn, l2g_ref[...], l2b_ref[...])


def _outffn(oa, ob, x, w_out, g_a, ln1_g, ln1_b, sc_f, sh_f,
            w_up, cw, cb, w_down, g_f, ln2_g, ln2_b, alpha, ts):
    B, S, D = x.shape
    d_ff = w_down.shape[0]
    n_chunks = d_ff // FFN_CHUNK
    row = lambda b, i: (b, i, 0)
    const = lambda b, i: (0, 0)
    vec = pl.BlockSpec((1, D), const)
    vec_b = pl.BlockSpec((None, 1, D), lambda b, i: (b, 0, 0))
    once = pl.Buffered(1)
    return pl.pallas_call(
        functools.partial(_outffn_kernel, alpha=alpha, n_chunks=n_chunks),
        grid=(B, S // ts),
        in_specs=[pl.BlockSpec((None, ts, WIDTH), row),
                  pl.BlockSpec((None, ts, WIDTH), row),
                  pl.BlockSpec((None, ts, D), row),
                  pl.BlockSpec((2 * WIDTH, D), const, pipeline_mode=once),
                  vec_b, vec, vec, vec_b, vec_b,
                  pl.BlockSpec((D, 2 * d_ff), const, pipeline_mode=once),
                  pl.BlockSpec((CONV_WIDTH, 2 * d_ff), const),
                  pl.BlockSpec((1, 2 * d_ff), const),
                  pl.BlockSpec((d_ff, D), const, pipeline_mode=once),
                  vec_b, vec, vec],
        out_specs=pl.BlockSpec((None, ts, D), row),
        out_shape=jax.ShapeDtypeStruct((B, S, D), F32),
        scratch_shapes=[pltpu.VMEM((HALO + ts, D), BF16),
                        pltpu.VMEM((2, 2, HALO + ts, FFN_CHUNK), F32),
                        pltpu.VMEM((ts, d_ff), BF16),
                        pltpu.VMEM((ts, D), F32)],
        compiler_params=_cparams(("parallel", "arbitrary")),
        name="outffn",
    )(oa, ob, x, w_out, g_a, ln1_g, ln1_b, sc_f, sh_f, w_up, cw, cb, w_down, g_f, ln2_g, ln2_b)


def kernel(x, c, positions, w_ada, b_ada, w_in, b_fgate, gn_a, gn_b, w_out,
           ln1_g, ln1_b, w_up, conv_w, conv_b, w_down, ln2_g, ln2_b):
    B, S, D = x.shape
    depth = w_ada.shape[0]
    d_ff = w_down.shape[1]
    alpha = (2.0 * depth) ** 0.25
    scale = HEAD_DIM ** -0.5
    ts = 512
    tq, tk = 512, 256

    for l in range(depth):
        ada = _ada(c, w_ada[l], b_ada[l])
        sh_a, sc_a, g_a, sh_f, sc_f, g_f = (t[:, None, :] for t in jnp.split(ada, 6, axis=-1))

        w = w_in[l]
        o3 = 3 * WIDTH
        w_f = jnp.pad(w[:, o3:o3 + N_HEADS], ((0, 0), (0, F_PAD - N_HEADS)))
        ob = o3 + N_HEADS
        w_cat = jnp.concatenate(
            [w[:, 0:WIDTH] * scale, w[:, WIDTH:o3],
             w[:, ob:ob + WIDTH] * scale, w[:, ob + WIDTH:ob + 3 * WIDTH], w_f], axis=1).astype(BF16)

        qa, ka, va, qb, kb, vb, f_t = _inproj(x, sc_a, sh_a, w_cat, positions, ts)

        f_cum = _fcumsum(f_t.reshape(B * N_HEADS, S), jnp.tile(b_fgate[l], B).reshape(B * N_HEADS, 1))
        f_blocks = f_cum.reshape(B, N_HEADS, S // tk, 1, tk)

        oa = _fox(qa, ka, va, f_blocks, gn_a[l].reshape(1, WIDTH), tq, tk)
        ob_ = _dilated(qb, kb, vb, gn_b[l].reshape(1, WIDTH))

        x = _outffn(oa, ob_, x, w_out[l].astype(BF16), g_a,
                    ln1_g[l].reshape(1, D), ln1_b[l].reshape(1, D), sc_f, sh_f,
                    w_up[l].astype(BF16), conv_w[l], conv_b[l].reshape(1, 2 * d_ff),
                    w_down[l].astype(BF16), g_f,
                    ln2_g[l].reshape(1, D), ln2_b[l].reshape(1, D), alpha, ts)
    return x
```
